```python
import math
import jax, jax.numpy as jnp
from jax import lax
import numpy as np

D_MODEL = 1024
BATCH = 8
SEQ = 4096
DEPTH = 2

CHUNK = 64
D_FF = 2816
FFN_RES = 0.5
D_CONV = D_MODEL // 2
CONV_A_WIDTH = 31
N_HEADS_B = 8
HEAD_DIM = 64
D_ATTN = N_HEADS_B * HEAD_DIM
Q_BLOCK = 128
CONV_C_WIDTH = 3
D_SHORT = D_MODEL
N_EVEN = (DEPTH + 1) // 2
N_ODD = DEPTH // 2
D_IN_EVEN = 2 * D_CONV + 3 * D_ATTN + N_HEADS_B
D_IN_ODD = 3 * D_SHORT
EPS = 1e-6

kernel_name = "hybrid_conformer_fox_shortconv_trunk"


def rmsnorm(x, g):
    xf = x.astype(jnp.float32)
    y = xf * lax.rsqrt(jnp.mean(xf * xf, axis=-1, keepdims=True) + EPS)
    return (y * g.astype(jnp.float32)).astype(x.dtype)


def swiglu(x, w_gate, w_up, w_down):
    return (jax.nn.silu(x @ w_gate) * (x @ w_up)) @ w_down


def causal_depthwise_conv(x, w):
    k_width = w.shape[0]
    return lax.conv_general_dilated(
        x, w[:, None, :].astype(x.dtype), window_strides=(1,),
        padding=[(k_width - 1, 0)],
        dimension_numbers=("NWC", "WIO", "NWC"),
        feature_group_count=x.shape[-1])


def forgetting_attention(q, k, v, log_f):
    b, s_len, h, dh = q.shape
    nb = s_len // Q_BLOCK
    scale = 1.0 / math.sqrt(dh)
    cum = jnp.cumsum(log_f, axis=1).transpose(0, 2, 1)
    qh = q.transpose(0, 2, 1, 3)
    kh = k.transpose(0, 2, 1, 3)
    vh = v.transpose(0, 2, 1, 3)
    q_blocks = qh.reshape(b, h, nb, Q_BLOCK, dh).transpose(2, 0, 1, 3, 4)
    f_blocks = cum.reshape(b, h, nb, Q_BLOCK).transpose(2, 0, 1, 3)
    q_pos = jnp.arange(s_len).reshape(nb, Q_BLOCK)
    k_pos = jnp.arange(s_len)

    def block(args):
        qi, fi, pi = args
        logits = jnp.einsum("bhqd,bhkd->bhqk", qi, kh,
                            preferred_element_type=jnp.float32) * scale
        logits = logits + fi[..., None] - cum[:, :, None, :]
        logits = jnp.where(k_pos[None, :] <= pi[:, None], logits, -jnp.inf)
        p = jax.nn.softmax(logits, axis=-1)
        return jnp.einsum("bhqk,bhkd->bhqd", p.astype(vh.dtype), vh)

    o = lax.map(block, (q_blocks, f_blocks, q_pos))
    return o.transpose(1, 0, 3, 2, 4).reshape(b, s_len, h * dh)


def even_mixer(h, w_in, b_f, conv_w, conv_b, conv_norm, q_norm, k_norm, w_out):
    b, s_len, _ = h.shape
    z = h @ w_in
    splits = np.cumsum([D_CONV, D_CONV, D_ATTN, D_ATTN, D_ATTN]).tolist()
    u, g, q, k, v, f_logit = jnp.split(z, splits, axis=-1)
    a = u * jax.nn.sigmoid(g)
    a = causal_depthwise_conv(a, conv_w) + conv_b
    a = jax.nn.silu(rmsnorm(a, conv_norm))
    q = rmsnorm(q.reshape(b, s_len, N_HEADS_B, HEAD_DIM), q_norm)
    k = rmsnorm(k.reshape(b, s_len, N_HEADS_B, HEAD_DIM), k_norm)
    v = v.reshape(b, s_len, N_HEADS_B, HEAD_DIM)
    log_f = jax.nn.log_sigmoid(f_logit.astype(jnp.float32) + b_f.astype(jnp.float32))
    o = forgetting_attention(q, k, v, log_f)
    return jnp.concatenate([a, o.astype(a.dtype)], axis=-1) @ w_out


def odd_mixer(h, w_in, conv_w, w_out):
    z = h @ w_in
    gate_b, gate_c, hh = jnp.split(z, 3, axis=-1)
    y = gate_b * causal_depthwise_conv(gate_c * hh, conv_w)
    return y @ w_out


def setup_inputs(seed: int = 0) -> dict:
    key = jax.random.key(seed)
    ks = iter(jax.random.split(key, 32))
    f32 = jnp.float32

    def nrm(shape, fan_in):
        return jax.random.normal(next(ks), shape, f32) * (fan_in ** -0.5)

    def gain(shape):
        return 1.0 + 0.02 * jax.random.normal(next(ks), shape, f32)

    return {
        "x": jax.random.normal(next(ks), (BATCH, SEQ, D_MODEL), f32),
        "ffn1_norm": gain((DEPTH, D_MODEL)),
        "ffn1_w_gate": nrm((DEPTH, D_MODEL, D_FF), D_MODEL),
        "ffn1_w_up": nrm((DEPTH, D_MODEL, D_FF), D_MODEL),
        "ffn1_w_down": nrm((DEPTH, D_FF, D_MODEL), D_FF),
        "mix_norm": gain((DEPTH, D_MODEL)),
        "ffn2_norm": gain((DEPTH, D_MODEL)),
        "ffn2_w_gate": nrm((DEPTH, D_MODEL, D_FF), D_MODEL),
        "ffn2_w_up": nrm((DEPTH, D_MODEL, D_FF), D_MODEL),
        "ffn2_w_down": nrm((DEPTH, D_FF, D_MODEL), D_FF),
        "ev_w_in": nrm((N_EVEN, D_MODEL, D_IN_EVEN), D_MODEL),
        "ev_b_f": jax.random.uniform(next(ks), (N_EVEN, N_HEADS_B), f32, 1.0, 5.0),
        "ev_conv_w": nrm((N_EVEN, CONV_A_WIDTH, D_CONV), CONV_A_WIDTH),
        "ev_conv_b": 0.01 * jax.random.normal(next(ks), (N_EVEN, D_CONV), f32),
        "ev_conv_norm": gain((N_EVEN, D_CONV)),
        "ev_q_norm": gain((N_EVEN, HEAD_DIM)),
        "ev_k_norm": gain((N_EVEN, HEAD_DIM)),
        "ev_w_out": nrm((N_EVEN, D_CONV + D_ATTN, D_MODEL), D_CONV + D_ATTN),
        "od_w_in": nrm((N_ODD, D_MODEL, D_IN_ODD), D_MODEL),
        "od_conv_w": nrm((N_ODD, CONV_C_WIDTH, D_SHORT), CONV_C_WIDTH),
        "od_w_out": nrm((N_ODD, D_SHORT, D_MODEL), D_SHORT),
    }


def reference(x, ffn1_norm, ffn1_w_gate, ffn1_w_up, ffn1_w_down, mix_norm,
              ffn2_norm, ffn2_w_gate, ffn2_w_up, ffn2_w_down,
              ev_w_in, ev_b_f, ev_conv_w, ev_conv_b, ev_conv_norm, ev_q_norm,
              ev_k_norm, ev_w_out, od_w_in, od_conv_w, od_w_out):
    for layer in range(DEPTH):
        x = x + FFN_RES * swiglu(rmsnorm(x, ffn1_norm[layer]), ffn1_w_gate[layer],
                                 ffn1_w_up[layer], ffn1_w_down[layer])
        h = rmsnorm(x, mix_norm[layer])
        if layer % 2 == 0:
            i = layer // 2
            x = x + even_mixer(h, ev_w_in[i], ev_b_f[i], ev_conv_w[i], ev_conv_b[i],
                               ev_conv_norm[i], ev_q_norm[i], ev_k_norm[i], ev_w_out[i])
        else:
            i = layer // 2
            x = x + odd_mixer(h, od_w_in[i], od_conv_w[i], od_w_out[i])
        x = x + FFN_RES * swiglu(rmsnorm(x, ffn2_norm[layer]), ffn2_w_gate[layer],
                                 ffn2_w_up[layer], ffn2_w_down[layer])
    return x
```

```python
import functools
import math

import jax
import jax.numpy as jnp
from jax import lax
from jax.experimental import pallas as pl
from jax.experimental.pallas import tpu as pltpu

F32 = jnp.float32
BF16 = jnp.bfloat16

D_MODEL = 1024
D_FF = 2816
FFN_RES = 0.5
D_CONV = 512
CONV_A_WIDTH = 31
N_HEADS = 8
HEAD_DIM = 64
D_ATTN = N_HEADS * HEAD_DIM
CONV_C_WIDTH = 3
D_SHORT = 1024
EPS = 1e-6

LANES = 128
F_PAD = LANES
D_IN_EVEN_PAD = 2 * D_CONV + 3 * D_ATTN + F_PAD

TM_FFN = 512
FF_CHUNK = 256
TM_MIX = 512
HALO_A = 32
HALO_C = 8
BQ = 512
BK = 512
VMEM_LIMIT = 56 * 1024 * 1024


def _const_spec(shape):
    nd = len(shape)
    return pl.BlockSpec(shape, lambda *_: (0,) * nd, pipeline_mode=pl.Buffered(1))


def _rms_scale(x):
    return lax.rsqrt(jnp.mean(x * x, axis=-1, keepdims=True) + EPS)


def _ffn_kernel(x_ref, g_ref, wg_ref, wu_ref, wd_ref, o_ref, h_ref):
    x = x_ref[...]
    n = (x * _rms_scale(x) * g_ref[...]).astype(BF16)
    for c in range(D_FF // FF_CHUNK):
        sl = slice(c * FF_CHUNK, (c + 1) * FF_CHUNK)
        g = jnp.dot(n, wg_ref[:, sl], preferred_element_type=F32)
        u = jnp.dot(n, wu_ref[:, sl], preferred_element_type=F32)
        h_ref[:, sl] = (g * jax.nn.sigmoid(g) * u).astype(BF16)
    y = jnp.dot(h_ref[...], wd_ref[...], preferred_element_type=F32)
    o_ref[...] = x + FFN_RES * y


def _ffn(x2d, gain, w_gate, w_up, w_down):
    n_tok = x2d.shape[0]
    return pl.pallas_call(
        _ffn_kernel,
        grid=(n_tok // TM_FFN,),
        in_specs=[
            pl.BlockSpec((TM_FFN, D_MODEL), lambda i: (i, 0)),
            _const_spec((1, D_MODEL)),
            _const_spec((D_MODEL, D_FF)),
            _const_spec((D_MODEL, D_FF)),
            _const_spec((D_FF, D_MODEL)),
        ],
        out_specs=pl.BlockSpec((TM_FFN, D_MODEL), lambda i: (i, 0)),
        out_shape=jax.ShapeDtypeStruct((n_tok, D_MODEL), F32),
        scratch_shapes=[pltpu.VMEM((TM_FFN, D_FF), BF16)],
        compiler_params=pltpu.CompilerParams(
            dimension_semantics=("arbitrary",), vmem_limit_bytes=VMEM_LIMIT),
        name="ffn",
    )(x2d, gain.reshape(1, D_MODEL), w_gate.astype(BF16), w_up.astype(BF16), w_down.astype(BF16))


def _split3_bf16(x):
    hi = x.astype(BF16)
    r1 = x - hi.astype(F32)
    mid = r1.astype(BF16)
    lo = (r1 - mid.astype(F32)).astype(BF16)
    return hi, mid, lo


def _even_in_kernel(x_ref, g_ref, w_ref, bf_ref, cw_ref, cb_ref, cn_ref, qg_ref, kg_ref, grp_ref,
                    tri_ref, a_ref, q_ref, k_ref, v_ref, ft_ref, abuf, fcarry):
    i = pl.program_id(1)
    tm = x_ref.shape[1]
    x = x_ref[0]
    h = (x * _rms_scale(x) * g_ref[...]).astype(BF16)

    def proj(lo, width):
        return jnp.dot(h, w_ref[:, lo:lo + width], preferred_element_type=F32)

    u = proj(0, D_CONV)
    gate = proj(D_CONV, D_CONV)

    @pl.when(i == 0)
    def _():
        abuf[0:HALO_A, :] = jnp.zeros((HALO_A, D_CONV), F32)
        fcarry[...] = jnp.zeros_like(fcarry)

    @pl.when(i > 0)
    def _():
        abuf[0:HALO_A, :] = abuf[tm:tm + HALO_A, :]

    abuf[HALO_A:HALO_A + tm, :] = u * jax.nn.sigmoid(gate)
    base = HALO_A - (CONV_A_WIDTH - 1)
    conv = jnp.zeros((tm, D_CONV), F32) + cb_ref[...]
    for t in range(CONV_A_WIDTH):
        conv = conv + cw_ref[t:t + 1, :] * abuf[base + t:base + t + tm, :]
    an = conv * _rms_scale(conv) * cn_ref[...]
    a_ref[0] = (an * jax.nn.sigmoid(an)).astype(BF16)

    def head_norm(z, gain_ref):
        ss = jnp.dot((z * z).astype(BF16), grp_ref[...], preferred_element_type=F32)
        return (z * lax.rsqrt(ss * (1.0 / HEAD_DIM) + EPS) * gain_ref[...]).astype(BF16)

    q_ref[0] = head_norm(proj(2 * D_CONV, D_ATTN), qg_ref)
    k_ref[0] = head_norm(proj(2 * D_CONV + D_ATTN, D_ATTN), kg_ref)
    v_ref[0] = proj(2 * D_CONV + 2 * D_ATTN, D_ATTN).astype(BF16)

    fl = proj(2 * D_CONV + 3 * D_ATTN, F_PAD) + bf_ref[...]
    logf = jnp.minimum(fl, 0.0) - jnp.log1p(jnp.exp(-jnp.abs(fl)))
    hi, mid, lo = _split3_bf16(logf)
    tri = tri_ref[...]
    cum = (jnp.dot(tri, hi, preferred_element_type=F32)
           + jnp.dot(tri, mid, preferred_element_type=F32)
           + jnp.dot(tri, lo, preferred_element_type=F32)) + fcarry[...]
    fcarry[...] = cum[tm - 1:tm, :]
    ft_ref[0] = cum.T[0:N_HEADS, :]


def _even_in(x, gain, w_in, b_f, conv_w, conv_b, conv_norm, q_norm, k_norm):
    bsz, seq, _ = x.shape
    tm = TM_MIX
    scale = 1.0 / math.sqrt(HEAD_DIM)
    w_pad = jnp.pad(w_in, ((0, 0), (0, F_PAD - N_HEADS))).astype(BF16)
    bf_pad = jnp.pad(b_f, (0, F_PAD - N_HEADS)).reshape(1, F_PAD)
    cw_pad = jnp.pad(conv_w, ((0, 1), (0, 0)))
    qg = (jnp.tile(q_norm, N_HEADS) * scale).reshape(1, D_ATTN)
    kg = jnp.tile(k_norm, N_HEADS).reshape(1, D_ATTN)
    head_of = jnp.arange(D_ATTN) // HEAD_DIM
    grp = (head_of[:, None] == head_of[None, :]).astype(BF16)
    tri = (jnp.arange(tm)[:, None] >= jnp.arange(tm)[None, :]).astype(BF16)
    tok = lambda b, i: (b, i, 0)
    return pl.pallas_call(
        _even_in_kernel,
        grid=(bsz, seq // tm),
        in_specs=[
            pl.BlockSpec((1, tm, D_MODEL), tok),
            _const_spec((1, D_MODEL)),
            _const_spec((D_MODEL, D_IN_EVEN_PAD)),
            _const_spec((1, F_PAD)),
            _const_spec((CONV_A_WIDTH + 1, D_CONV)),
            _const_spec((1, D_CONV)),
            _const_spec((1, D_CONV)),
            _const_spec((1, D_ATTN)),
            _const_spec((1, D_ATTN)),
            _const_spec((D_ATTN, D_ATTN)),
            _const_spec((tm, tm)),
        ],
        out_specs=[
            pl.BlockSpec((1, tm, D_CONV), tok),
            pl.BlockSpec((1, tm, D_ATTN), tok),
            pl.BlockSpec((1, tm, D_ATTN), tok),
            pl.BlockSpec((1, tm, D_ATTN), tok),
            pl.BlockSpec((1, N_HEADS, tm), lambda b, i: (b, 0, i)),
        ],
        out_shape=[
            jax.ShapeDtypeStruct((bsz, seq, D_CONV), BF16),
            jax.ShapeDtypeStruct((bsz, seq, D_ATTN), BF16),
            jax.ShapeDtypeStruct((bsz, seq, D_ATTN), BF16),
            jax.ShapeDtypeStruct((bsz, seq, D_ATTN), BF16),
            jax.ShapeDtypeStruct((bsz, N_HEADS, seq), F32),
        ],
        scratch_shapes=[pltpu.VMEM((tm + HALO_A, D_CONV), F32), pltpu.VMEM((1, F_PAD), F32)],
        compiler_params=pltpu.CompilerParams(
            dimension_semantics=("arbitrary", "arbitrary"), vmem_limit_bytes=VMEM_LIMIT),
        name="even_in",
    )(x, gain.reshape(1, D_MODEL), w_pad, bf_pad, cw_pad, conv_b.reshape(1, D_CONV),
      conv_norm.reshape(1, D_CONV), qg, kg, grp, tri)


def _attn_kernel(q_ref, k_ref, v_ref, fq_ref, fk_ref, o_ref, m_ref, l_ref, acc_ref):
    i = pl.program_id(2)
    lane = lax.broadcasted_iota(jnp.int32, (1, LANES), 1)
    first = lane < HEAD_DIM
    q2 = q_ref[0]
    zero = jnp.zeros_like(q2)
    qq = jnp.concatenate([jnp.where(first, q2, zero), jnp.where(first, zero, q2)], axis=0)
    f0 = fq_ref[0, 0, :, 0:1]
    m_ref[...] = jnp.full_like(m_ref, -jnp.inf)
    l_ref[...] = jnp.zeros_like(l_ref)
    acc_ref[...] = jnp.zeros_like(acc_ref)

    def block(j, masked):
        start = pl.multiple_of(j * BK, BK)
        kj = k_ref[0, pl.ds(start, BK), :]
        vj = v_ref[0, pl.ds(start, BK), :]
        s = lax.dot_general(qq, kj, (((1,), (1,)), ((), ())), preferred_element_type=F32)
        bias = f0 - fk_ref[0, 0, :, pl.ds(start, BK)]
        s = jnp.concatenate([s[:BQ] + bias[0:1], s[BQ:] + bias[1:2]], axis=0)
        if masked:
            row = lax.broadcasted_iota(jnp.int32, (BQ, BK), 0)
            col = lax.broadcasted_iota(jnp.int32, (BQ, BK), 1)
            keep = jnp.concatenate([col <= row, col <= row], axis=0)
            s = jnp.where(keep, s, -jnp.inf)
        m_prev = m_ref[...]
        m_new = jnp.maximum(m_prev, jnp.max(s, axis=-1, keepdims=True))
        alpha = jnp.exp(m_prev - m_new)
        p = jnp.exp(s - m_new)
        l_ref[...] = alpha * l_ref[...] + jnp.sum(p, axis=-1, keepdims=True)
        m_ref[...] = m_new
        pb = p.astype(BF16)
        pv0 = jnp.dot(pb[:BQ], vj, preferred_element_type=F32)
        pv1 = jnp.dot(pb[BQ:], vj, preferred_element_type=F32)
        a_sel = jnp.where(first, alpha[:BQ], alpha[BQ:])
        acc_ref[...] = acc_ref[...] * a_sel + jnp.where(first, pv0, pv1)

    def body(j, carry):
        block(j, masked=False)
        return carry

    lax.fori_loop(0, i, body, 0)
    block(i, masked=True)
    l = l_ref[...]
    inv = jnp.where(first, 1.0 / l[:BQ], 1.0 / l[BQ:])
    o_ref[0] = (acc_ref[...] * inv).astype(BF16)


def _attention(q, k, v, ft):
    bsz, seq, _ = q.shape
    n_pairs = N_HEADS // 2
    f4 = ft.reshape(bsz, n_pairs, 2, seq)
    return pl.pallas_call(
        _attn_kernel,
        grid=(bsz, n_pairs, seq // BQ),
        in_specs=[
            pl.BlockSpec((1, BQ, LANES), lambda b, p, i: (b, i, p)),
            pl.BlockSpec((1, seq, LANES), lambda b, p, i: (b, 0, p)),
            pl.BlockSpec((1, seq, LANES), lambda b, p, i: (b, 0, p)),
            pl.BlockSpec((1, 1, 2, BQ), lambda b, p, i: (b, p, 0, i)),
            pl.BlockSpec((1, 1, 2, seq), lambda b, p, i: (b, p, 0, 0)),
        ],
        out_specs=pl.BlockSpec((1, BQ, LANES), lambda b, p, i: (b, i, p)),
        out_shape=jax.ShapeDtypeStruct((bsz, seq, D_ATTN), BF16),
        scratch_shapes=[
            pltpu.VMEM((2 * BQ, 1), F32),
            pltpu.VMEM((2 * BQ, 1), F32),
            pltpu.VMEM((BQ, LANES), F32),
        ],
        compiler_params=pltpu.CompilerParams(
            dimension_semantics=("arbitrary", "arbitrary", "arbitrary"),
            vmem_limit_bytes=VMEM_LIMIT),
        name="fox_attention",
    )(q, k, v, f4, f4)


def _even_out_kernel(x_ref, a_ref, o_ref, wa_ref, wo_ref, y_ref):
    y = jnp.dot(a_ref[...], wa_ref[...], preferred_element_type=F32)
    y = y + jnp.dot(o_ref[...], wo_ref[...], preferred_element_type=F32)
    y_ref[...] = x_ref[...] + y


def _even_out(x2d, a2d, o2d, w_out):
    n_tok = x2d.shape[0]
    tm = TM_MIX
    w = w_out.astype(BF16)
    row = lambda i: (i, 0)
    return pl.pallas_call(
        _even_out_kernel,
        grid=(n_tok // tm,),
        in_specs=[
            pl.BlockSpec((tm, D_MODEL), row),
            pl.BlockSpec((tm, D_CONV), row),
            pl.BlockSpec((tm, D_ATTN), row),
            _const_spec((D_CONV, D_MODEL)),
            _const_spec((D_ATTN, D_MODEL)),
        ],
        out_specs=pl.BlockSpec((tm, D_MODEL), row),
        out_shape=jax.ShapeDtypeStruct((n_tok, D_MODEL), F32),
        compiler_params=pltpu.CompilerParams(
            dimension_semantics=("arbitrary",), vmem_limit_bytes=VMEM_LIMIT),
        name="even_out",
    )(x2d, a2d, o2d, w[:D_CONV], w[D_CONV:])


def _odd_kernel(x_ref, g_ref, wi_ref, cw_ref, wo_ref, y_ref, mbuf):
    i = pl.program_id(1)
    tm = x_ref.shape[1]
    x = x_ref[0]
    h = (x * _rms_scale(x) * g_ref[...]).astype(BF16)

    def proj(k):
        return jnp.dot(h, wi_ref[:, k * D_SHORT:(k + 1) * D_SHORT], preferred_element_type=F32)

    @pl.when(i == 0)
    def _():
        mbuf[0:HALO_C, :] = jnp.zeros((HALO_C, D_SHORT), F32)

    @pl.when(i > 0)
    def _():
        mbuf[0:HALO_C, :] = mbuf[tm:tm + HALO_C, :]

    mbuf[HALO_C:HALO_C + tm, :] = proj(1) * proj(2)
    base = HALO_C - (CONV_C_WIDTH - 1)
    conv = cw_ref[0:1, :] * mbuf[base:base + tm, :]
    for t in range(1, CONV_C_WIDTH):
        conv = conv + cw_ref[t:t + 1, :] * mbuf[base + t:base + t + tm, :]
    y = (proj(0) * conv).astype(BF16)
    y_ref[0] = x + jnp.dot(y, wo_ref[...], preferred_element_type=F32)


def _odd_mixer(x, gain, w_in, conv_w, w_out):
    bsz, seq, _ = x.shape
    tm = TM_MIX
    tok = lambda b, i: (b, i, 0)
    return pl.pallas_call(
        _odd_kernel,
        grid=(bsz, seq // tm),
        in_specs=[
            pl.BlockSpec((1, tm, D_MODEL), tok),
            _const_spec((1, D_MODEL)),
            _const_spec((D_MODEL, 3 * D_SHORT)),
            _const_spec((CONV_C_WIDTH + 1, D_SHORT)),
            _const_spec((D_SHORT, D_MODEL)),
        ],
        out_specs=pl.BlockSpec((1, tm, D_MODEL), tok),
        out_shape=jax.ShapeDtypeStruct((bsz, seq, D_MODEL), F32),
        scratch_shapes=[pltpu.VMEM((tm + HALO_C, D_SHORT), F32)],
        compiler_params=pltpu.CompilerParams(
            dimension_semantics=("arbitrary", "arbitrary"), vmem_limit_bytes=VMEM_LIMIT),
        name="odd_mixer",
    )(x, gain.reshape(1, D_MODEL), w_in.astype(BF16), jnp.pad(conv_w, ((0, 1), (0, 0))),
      w_out.astype(BF16))


def kernel(x, ffn1_norm, ffn1_w_gate, ffn1_w_up, ffn1_w_down, mix_norm, ffn2_norm, ffn2_w_gate,
           ffn2_w_up, ffn2_w_down, ev_w_in, ev_b_f, ev_conv_w, ev_conv_b, ev_conv_norm, ev_q_norm,
           ev_k_norm, ev_w_out, od_w_in, od_conv_w, od_w_out):
    bsz, seq, d = x.shape
    n_tok = bsz * seq
    depth = ffn1_norm.shape[0]
    for layer in range(depth):
        x = _ffn(x.reshape(n_tok, d), ffn1_norm[layer], ffn1_w_gate[layer], ffn1_w_up[layer],
                 ffn1_w_down[layer]).reshape(bsz, seq, d)
        i = layer // 2
        if layer % 2 == 0:
            a, q, k, v, ft = _even_in(x, mix_norm[layer], ev_w_in[i], ev_b_f[i], ev_conv_w[i],
                                      ev_conv_b[i], ev_conv_norm[i], ev_q_norm[i], ev_k_norm[i])
            o = _attention(q, k, v, ft)
            x = _even_out(x.reshape(n_tok, d), a.reshape(n_tok, D_CONV), o.reshape(n_tok, D_ATTN),
                          ev_w_out[i]).reshape(bsz, seq, d)
        else:
            x = _odd_mixer(x, mix_norm[layer], od_w_in[i], od_conv_w[i], od_w_out[i])
        x = _ffn(x.reshape(n_tok, d), ffn2_norm[layer], ffn2_w_gate[layer], ffn2_w_up[layer],
                 ffn2_w_down[layer]).reshape(bsz, seq, d)
    return x
```

```python
import math

import jax
import jax.numpy as jnp
import numpy as np
from jax import lax
from jax.experimental import pallas as pl
from jax.experimental.pallas import tpu as pltpu

F32 = jnp.float32
BF16 = jnp.bfloat16

D_MODEL = 1024
D_FF = 2816
FFN_RES = 0.5
D_CONV = 512
CONV_A_WIDTH = 31
N_HEADS = 8
HEAD_DIM = 64
D_ATTN = N_HEADS * HEAD_DIM
CONV_C_WIDTH = 3
D_SHORT = 1024
EPS = 1e-6

LANES = 128
F_PAD = LANES
D_IN_EVEN_PAD = 2 * D_CONV + 3 * D_ATTN + F_PAD

TM_FFN = 512
FF_CHUNK = 256
TM_MIX = 512
HALO_A = 32
HALO_C = 8
BQ = 512
HQ = 256
BK = 512
KX_W = 2 * LANES
VMEM_LIMIT = 56 * 1024 * 1024


def _const_spec(shape):
    nd = len(shape)
    return pl.BlockSpec(shape, lambda *_: (0,) * nd, pipeline_mode=pl.Buffered(1))


def _rms_scale(x):
    return lax.rsqrt(jnp.mean(x * x, axis=-1, keepdims=True) + EPS)


def _ffn_kernel(x_ref, g_ref, wg_ref, wu_ref, wd_ref, o_ref, h_ref):
    x = x_ref[...]
    n = (x * _rms_scale(x) * g_ref[...]).astype(BF16)
    for c in range(D_FF // FF_CHUNK):
        sl = slice(c * FF_CHUNK, (c + 1) * FF_CHUNK)
        g = jnp.dot(n, wg_ref[:, sl], preferred_element_type=F32)
        u = jnp.dot(n, wu_ref[:, sl], preferred_element_type=F32)
        h_ref[:, sl] = (g * jax.nn.sigmoid(g) * u).astype(BF16)
    y = jnp.dot(h_ref[...], wd_ref[...], preferred_element_type=F32)
    o_ref[...] = x + FFN_RES * y


def _ffn(x2d, gain, w_gate, w_up, w_down):
    n_tok = x2d.shape[0]
    return pl.pallas_call(
        _ffn_kernel,
        grid=(n_tok // TM_FFN,),
        in_specs=[
            pl.BlockSpec((TM_FFN, D_MODEL), lambda i: (i, 0)),
            _const_spec((1, D_MODEL)),
            _const_spec((D_MODEL, D_FF)),
            _const_spec((D_MODEL, D_FF)),
            _const_spec((D_FF, D_MODEL)),
        ],
        out_specs=pl.BlockSpec((TM_FFN, D_MODEL), lambda i: (i, 0)),
        out_shape=jax.ShapeDtypeStruct((n_tok, D_MODEL), F32),
        scratch_shapes=[pltpu.VMEM((TM_FFN, D_FF), BF16)],
        compiler_params=pltpu.CompilerParams(
            dimension_semantics=("arbitrary",), vmem_limit_bytes=VMEM_LIMIT),
        name="ffn",
    )(x2d, gain.reshape(1, D_MODEL), w_gate.astype(BF16), w_up.astype(BF16), w_down.astype(BF16))


def _split3_bf16(x):
    hi = x.astype(BF16)
    r1 = x - hi.astype(F32)
    mid = r1.astype(BF16)
    lo = (r1 - mid.astype(F32)).astype(BF16)
    return hi, mid, lo


def _even_in_kernel(x_ref, g_ref, w_ref, bf_ref, cw_ref, cb_ref, cn_ref, qg_ref, kg_ref, grp_ref,
                    tri_ref, sel_ref, a_ref, qt_ref, kx_ref, vx_ref, abuf, fcarry):
    i = pl.program_id(1)
    tm = x_ref.shape[1]
    x = x_ref[0]
    h = (x * _rms_scale(x) * g_ref[...]).astype(BF16)

    def proj(lo, width):
        return jnp.dot(h, w_ref[:, lo:lo + width], preferred_element_type=F32)

    u = proj(0, D_CONV)
    gate = proj(D_CONV, D_CONV)

    @pl.when(i == 0)
    def _():
        abuf[0:HALO_A, :] = jnp.zeros((HALO_A, D_CONV), F32)
        fcarry[...] = jnp.zeros_like(fcarry)

    @pl.when(i > 0)
    def _():
        abuf[0:HALO_A, :] = abuf[tm:tm + HALO_A, :]

    abuf[HALO_A:HALO_A + tm, :] = u * jax.nn.sigmoid(gate)
    base = HALO_A - (CONV_A_WIDTH - 1)
    conv = jnp.zeros((tm, D_CONV), F32) + cb_ref[...]
    for t in range(CONV_A_WIDTH):
        conv = conv + cw_ref[t:t + 1, :] * abuf[base + t:base + t + tm, :]
    an = conv * _rms_scale(conv) * cn_ref[...]
    a_ref[0] = (an * jax.nn.sigmoid(an)).astype(BF16)

    def head_norm(z, gain_ref):
        ss = jnp.dot((z * z).astype(BF16), grp_ref[...], preferred_element_type=F32)
        return z * lax.rsqrt(ss * (1.0 / HEAD_DIM) + EPS) * gain_ref[...]

    qt_ref[0] = head_norm(proj(2 * D_CONV, D_ATTN), qg_ref).T.astype(BF16)
    kn = head_norm(proj(2 * D_CONV + D_ATTN, D_ATTN), kg_ref).astype(BF16)
    vt = proj(2 * D_CONV + 2 * D_ATTN, D_ATTN).T
    ones = jnp.ones((HEAD_DIM, tm), BF16)
    for hd in range(N_HEADS):
        vx_ref[0, hd, 0, 0:HEAD_DIM, :] = vt[hd * HEAD_DIM:(hd + 1) * HEAD_DIM, :].astype(BF16)
        vx_ref[0, hd, 0, HEAD_DIM:2 * HEAD_DIM, :] = ones

    fl = proj(2 * D_CONV + 3 * D_ATTN, F_PAD) + bf_ref[...]
    logf = jnp.minimum(fl, 0.0) - jnp.log1p(jnp.exp(-jnp.abs(fl)))
    hi, mid, lo = _split3_bf16(logf)
    tri = tri_ref[...]
    cum = (jnp.dot(tri, hi, preferred_element_type=F32)
           + jnp.dot(tri, mid, preferred_element_type=F32)
           + jnp.dot(tri, lo, preferred_element_type=F32)) + fcarry[...]
    fcarry[...] = cum[tm - 1:tm, :]
    pieces = jnp.concatenate(_split3_bf16(cum), axis=1)
    ext = jnp.dot(pieces, sel_ref[...], preferred_element_type=F32).astype(BF16)
    for p in range(N_HEADS // 2):
        kx_ref[0, :, p * KX_W:p * KX_W + LANES] = kn[:, p * LANES:(p + 1) * LANES]
        kx_ref[0, :, p * KX_W + LANES:(p + 1) * KX_W] = ext[:, p * LANES:(p + 1) * LANES]


def _decay_selector():
    sel = np.zeros((3 * F_PAD, (N_HEADS // 2) * LANES), np.float32)
    for piece in range(3):
        for hd in range(N_HEADS):
            sel[piece * F_PAD + hd, (hd // 2) * LANES + 3 * (hd % 2) + piece] = 1.0
    return jnp.asarray(sel, BF16)


def _even_in(x, gain, w_in, b_f, conv_w, conv_b, conv_norm, q_norm, k_norm):
    bsz, seq, _ = x.shape
    tm = BK
    n_pairs = N_HEADS // 2
    scale = 1.0 / math.sqrt(HEAD_DIM)
    w_pad = jnp.pad(w_in, ((0, 0), (0, F_PAD - N_HEADS))).astype(BF16)
    bf_pad = jnp.pad(b_f, (0, F_PAD - N_HEADS)).reshape(1, F_PAD)
    cw_pad = jnp.pad(conv_w, ((0, 1), (0, 0)))
    qg = (jnp.tile(q_norm, N_HEADS) * scale).reshape(1, D_ATTN)
    kg = jnp.tile(k_norm, N_HEADS).reshape(1, D_ATTN)
    head_of = jnp.arange(D_ATTN) // HEAD_DIM
    grp = (head_of[:, None] == head_of[None, :]).astype(BF16)
    tri = (jnp.arange(tm)[:, None] >= jnp.arange(tm)[None, :]).astype(BF16)
    tok = lambda b, i: (b, i, 0)
    return pl.pallas_call(
        _even_in_kernel,
        grid=(bsz, seq // tm),
        in_specs=[
            pl.BlockSpec((1, tm, D_MODEL), tok),
            _const_spec((1, D_MODEL)),
            _const_spec((D_MODEL, D_IN_EVEN_PAD)),
            _const_spec((1, F_PAD)),
            _const_spec((CONV_A_WIDTH + 1, D_CONV)),
            _const_spec((1, D_CONV)),
            _const_spec((1, D_CONV)),
            _const_spec((1, D_ATTN)),
            _const_spec((1, D_ATTN)),
            _const_spec((D_ATTN, D_ATTN)),
            _const_spec((tm, tm)),
            _const_spec((3 * F_PAD, n_pairs * LANES)),
        ],
        out_specs=[
            pl.BlockSpec((1, tm, D_CONV), tok),
            pl.BlockSpec((1, D_ATTN, tm), lambda b, i: (b, 0, i)),
            pl.BlockSpec((1, tm, n_pairs * KX_W), tok),
            pl.BlockSpec((1, N_HEADS, 1, 2 * HEAD_DIM, tm), lambda b, i: (b, 0, i, 0, 0)),
        ],
        out_shape=[
            jax.ShapeDtypeStruct((bsz, seq, D_CONV), BF16),
            jax.ShapeDtypeStruct((bsz, D_ATTN, seq), BF16),
            jax.ShapeDtypeStruct((bsz, seq, n_pairs * KX_W), BF16),
            jax.ShapeDtypeStruct((bsz, N_HEADS, seq // tm, 2 * HEAD_DIM, tm), BF16),
        ],
        scratch_shapes=[pltpu.VMEM((tm + HALO_A, D_CONV), F32), pltpu.VMEM((1, F_PAD), F32)],
        compiler_params=pltpu.CompilerParams(
            dimension_semantics=("arbitrary", "arbitrary"), vmem_limit_bytes=VMEM_LIMIT),
        name="even_in",
    )(x, gain.reshape(1, D_MODEL), w_pad, bf_pad, cw_pad, conv_b.reshape(1, D_CONV),
      conv_norm.reshape(1, D_CONV), qg, kg, grp, tri, _decay_selector())


def _attn_kernel(qt_ref, kx_ref, vx_ref, o_ref, m_ref, acc_ref):
    i = pl.program_id(2)
    qt = qt_ref[0]
    zeros = jnp.zeros((HEAD_DIM, BQ), BF16)
    r = lax.broadcasted_iota(jnp.int32, (LANES, BQ), 0)
    rhs = []
    for hd in range(2):
        ext = jnp.where((r >= 3 * hd) & (r < 3 * hd + 3), -1.0, 0.0).astype(BF16)
        top = [qt[0:HEAD_DIM], zeros] if hd == 0 else [zeros, qt[HEAD_DIM:]]
        rhs.append(jnp.concatenate(top + [ext], axis=0))
    m_ref[...] = jnp.full_like(m_ref, -jnp.inf)
    acc_ref[...] = jnp.zeros_like(acc_ref)

    def logits(hd, half, jblk, koff, klen, masked):
        kstart = pl.multiple_of(jblk * BK + koff, HQ)
        kx = kx_ref[0, pl.ds(kstart, klen), :]
        st = jnp.dot(kx, rhs[hd][:, half * HQ:(half + 1) * HQ], preferred_element_type=F32)
        if masked:
            key = lax.broadcasted_iota(jnp.int32, (klen, HQ), 0)
            qry = lax.broadcasted_iota(jnp.int32, (klen, HQ), 1)
            st = jnp.where(key <= qry, st, -jnp.inf)
        return st

    def softmax_step(st, hd, half):
        c = 2 * hd + half
        m_prev = m_ref[c]
        m_new = jnp.maximum(m_prev, jnp.max(st, axis=0, keepdims=True))
        m_ref[c] = m_new
        return jnp.exp(m_prev - m_new), jnp.exp(st - m_new).astype(BF16)

    def accumulate(alpha, pt, hd, half, jblk, koff, klen):
        c = 2 * hd + half
        vx = vx_ref[0, hd, jblk, :, koff:koff + klen]
        acc_ref[c] = acc_ref[c] * alpha + jnp.dot(vx, pt, preferred_element_type=F32)

    def run(units):
        sts = [logits(*u) for u in units]
        for st, (hd, half, jblk, koff, klen, _) in zip(sts, units):
            alpha, pt = softmax_step(st, hd, half)
            accumulate(alpha, pt, hd, half, jblk, koff, klen)

    def body(j, carry):
        run([(hd, half, j, 0, BK, False) for hd in range(2) for half in range(2)])
        return carry

    lax.fori_loop(0, i, body, 0)
    run([(hd, 0, i, 0, HQ, True) for hd in range(2)] + [(hd, 1, i, 0, HQ, False) for hd in range(2)])
    run([(hd, 1, i, HQ, HQ, True) for hd in range(2)])
    for half in range(2):
        parts = []
        for hd in range(2):
            acc = acc_ref[2 * hd + half]
            parts.append(acc[0:HEAD_DIM] * (1.0 / acc[HEAD_DIM:HEAD_DIM + 1]))
        o_ref[0, half * HQ:(half + 1) * HQ, :] = jnp.concatenate(parts, axis=0).T.astype(BF16)


def _attention(qt, kx, vx):
    bsz, _, seq = qt.shape
    n_pairs = N_HEADS // 2
    return pl.pallas_call(
        _attn_kernel,
        grid=(bsz, n_pairs, seq // BQ),
        in_specs=[
            pl.BlockSpec((1, 2 * HEAD_DIM, BQ), lambda b, p, i: (b, p, i)),
            pl.BlockSpec((1, seq, KX_W), lambda b, p, i: (b, 0, p)),
            pl.BlockSpec((1, 2, seq // BK, 2 * HEAD_DIM, BK), lambda b, p, i: (b, p, 0, 0, 0)),
        ],
        out_specs=pl.BlockSpec((1, BQ, LANES), lambda b, p, i: (b, i, p)),
        out_shape=jax.ShapeDtypeStruct((bsz, seq, D_ATTN), BF16),
        scratch_shapes=[
            pltpu.VMEM((4, 1, HQ), F32),
            pltpu.VMEM((4, 2 * HEAD_DIM, HQ), F32),
        ],
        compiler_params=pltpu.CompilerParams(
            dimension_semantics=("arbitrary", "arbitrary", "arbitrary"),
            vmem_limit_bytes=VMEM_LIMIT),
        name="fox_attention",
    )(qt, kx, vx)


def _even_out_kernel(x_ref, a_ref, o_ref, wa_ref, wo_ref, y_ref):
    y = jnp.dot(a_ref[...], wa_ref[...], preferred_element_type=F32)
    y = y + jnp.dot(o_ref[...], wo_ref[...], preferred_element_type=F32)
    y_ref[...] = x_ref[...] + y


def _even_out(x2d, a2d, o2d, w_out):
    n_tok = x2d.shape[0]
    tm = TM_MIX
    w = w_out.astype(BF16)
    row = lambda i: (i, 0)
    return pl.pallas_call(
        _even_out_kernel,
        grid=(n_tok // tm,),
        in_specs=[
            pl.BlockSpec((tm, D_MODEL), row),
            pl.BlockSpec((tm, D_CONV), row),
            pl.BlockSpec((tm, D_ATTN), row),
            _const_spec((D_CONV, D_MODEL)),
            _const_spec((D_ATTN, D_MODEL)),
        ],
        out_specs=pl.BlockSpec((tm, D_MODEL), row),
        out_shape=jax.ShapeDtypeStruct((n_tok, D_MODEL), F32),
        compiler_params=pltpu.CompilerParams(
            dimension_semantics=("arbitrary",), vmem_limit_bytes=VMEM_LIMIT),
        name="even_out",
    )(x2d, a2d, o2d, w[:D_CONV], w[D_CONV:])


def _odd_kernel(x_ref, g_ref, wi_ref, cw_ref, wo_ref, y_ref, mbuf):
    i = pl.program_id(1)
    tm = x_ref.shape[1]
    x = x_ref[0]
    h = (x * _rms_scale(x) * g_ref[...]).astype(BF16)

    def proj(k):
        return jnp.dot(h, wi_ref[:, k * D_SHORT:(k + 1) * D_SHORT], preferred_element_type=F32)

    @pl.when(i == 0)
    def _():
        mbuf[0:HALO_C, :] = jnp.zeros((HALO_C, D_SHORT), F32)

    @pl.when(i > 0)
    def _():
        mbuf[0:HALO_C, :] = mbuf[tm:tm + HALO_C, :]

    mbuf[HALO_C:HALO_C + tm, :] = proj(1) * proj(2)
    base = HALO_C - (CONV_C_WIDTH - 1)
    conv = cw_ref[0:1, :] * mbuf[base:base + tm, :]
    for t in range(1, CONV_C_WIDTH):
        conv = conv + cw_ref[t:t + 1, :] * mbuf[base + t:base + t + tm, :]
    y = (proj(0) * conv).astype(BF16)
    y_ref[0] = x + jnp.dot(y, wo_ref[...], preferred_element_type=F32)


def _odd_mixer(x, gain, w_in, conv_w, w_out):
    bsz, seq, _ = x.shape
    tm = TM_MIX
    tok = lambda b, i: (b, i, 0)
    return pl.pallas_call(
        _odd_kernel,
        grid=(bsz, seq // tm),
        in_specs=[
            pl.BlockSpec((1, tm, D_MODEL), tok),
            _const_spec((1, D_MODEL)),
            _const_spec((D_MODEL, 3 * D_SHORT)),
            _const_spec((CONV_C_WIDTH + 1, D_SHORT)),
            _const_spec((D_SHORT, D_MODEL)),
        ],
        out_specs=pl.BlockSpec((1, tm, D_MODEL), tok),
        out_shape=jax.ShapeDtypeStruct((bsz, seq, D_MODEL), F32),
        scratch_shapes=[pltpu.VMEM((tm + HALO_C, D_SHORT), F32)],
        compiler_params=pltpu.CompilerParams(
            dimension_semantics=("arbitrary", "arbitrary"), vmem_limit_bytes=VMEM_LIMIT),
        name="odd_mixer",
    )(x, gain.reshape(1, D_MODEL), w_in.astype(BF16), jnp.pad(conv_w, ((0, 1), (0, 0))),
      w_out.astype(BF16))


def kernel(x, ffn1_norm, ffn1_w_gate, ffn1_w_up, ffn1_w_down, mix_norm, ffn2_norm, ffn2_w_gate,
           ffn2_w_up, ffn2_w_down, ev_w_in, ev_b_f, ev_conv_w, ev_conv_b, ev_conv_norm, ev_q_norm,
           ev_k_norm, ev_w_out, od_w_in, od_conv_w, od_w_out):
    bsz, seq, d = x.shape
    n_tok = bsz * seq
    depth = ffn1_norm.shape[0]
    for layer in range(depth):
        x = _ffn(x.reshape(n_tok, d), ffn1_norm[layer], ffn1_w_gate[layer], ffn1_w_up[layer],
                 ffn1_w_down[layer]).reshape(bsz, seq, d)
        i = layer // 2
        if layer % 2 == 0:
            a, qt, kx, vx = _even_in(x, mix_norm[layer], ev_w_in[i], ev_b_f[i], ev_conv_w[i],
                                     ev_conv_b[i], ev_conv_norm[i], ev_q_norm[i], ev_k_norm[i])
            o = _attention(qt, kx, vx)
            x = _even_out(x.reshape(n_tok, d), a.reshape(n_tok, D_CONV), o.reshape(n_tok, D_ATTN),
                          ev_w_out[i]).reshape(bsz, seq, d)
        else:
            x = _odd_mixer(x, mix_norm[layer], od_w_in[i], od_conv_w[i], od_w_out[i])
        x = _ffn(x.reshape(n_tok, d), ffn2_norm[layer], ffn2_w_gate[layer], ffn2_w_up[layer],
                 ffn2_w_down[layer]).reshape(bsz, seq, d)
    return x
```

```python
import functools
import math

import jax
import jax.numpy as jnp
import numpy as np
from jax import lax
from jax.experimental import pallas as pl
from jax.experimental.pallas import tpu as pltpu

F32 = jnp.float32
BF16 = jnp.bfloat16

D_MODEL = 1024
D_FF = 2816
FFN_RES = 0.5
D_CONV = 512
CONV_A_WIDTH = 31
N_HEADS = 8
HEAD_DIM = 64
D_ATTN = N_HEADS * HEAD_DIM
CONV_C_WIDTH = 3
D_SHORT = 1024
EPS = 1e-6

LANES = 128
SUBLANES = 8
F_PAD = LANES
D_IN_EVEN_PAD = 2 * D_CONV + 3 * D_ATTN + F_PAD

TM_FFN = 512
FF_CHUNK = 256
TM_MIX = 512
HALO_A = 32
HALO_C = 8
BQ = 512
KH = 256
PIPE_AHEAD = 3
LOG2E = 1.4426950408889634
BK = 512
KX_W = 2 * LANES
VMEM_LIMIT = 56 * 1024 * 1024


def _const_spec(shape):
    nd = len(shape)
    return pl.BlockSpec(shape, lambda *_: (0,) * nd, pipeline_mode=pl.Buffered(1))


def _rms_scale(x):
    return lax.rsqrt(jnp.mean(x * x, axis=-1, keepdims=True) + EPS)


def _ffn_kernel(*refs, mixer_out):
    if mixer_out:
        x_ref, a_ref, att_ref, wa_ref, wo_ref, g_ref, wg_ref, wu_ref, wd_ref, o_ref, h_ref = refs
        x = (x_ref[...] + jnp.dot(a_ref[...], wa_ref[...], preferred_element_type=F32)
             + jnp.dot(att_ref[...], wo_ref[...], preferred_element_type=F32))
    else:
        x_ref, g_ref, wg_ref, wu_ref, wd_ref, o_ref, h_ref = refs
        x = x_ref[...]
    n = (x * _rms_scale(x) * g_ref[...]).astype(BF16)
    for c in range(D_FF // FF_CHUNK):
        sl = slice(c * FF_CHUNK, (c + 1) * FF_CHUNK)
        g = jnp.dot(n, wg_ref[:, sl], preferred_element_type=F32)
        u = jnp.dot(n, wu_ref[:, sl], preferred_element_type=F32)
        h_ref[:, sl] = (g * jax.nn.sigmoid(g) * u).astype(BF16)
    y = jnp.dot(h_ref[...], wd_ref[...], preferred_element_type=F32)
    o_ref[...] = x + FFN_RES * y


def _ffn(x2d, gain, w_gate, w_up, w_down, mixer_out=None):
    n_tok = x2d.shape[0]
    row = lambda i: (i, 0)
    args = [x2d]
    in_specs = [pl.BlockSpec((TM_FFN, D_MODEL), row)]
    if mixer_out is not None:
        a2d, o2d, w_out = mixer_out
        w_out = w_out.astype(BF16)
        args += [a2d, o2d, w_out[:D_CONV], w_out[D_CONV:]]
        in_specs += [
            pl.BlockSpec((TM_FFN, D_CONV), row),
            pl.BlockSpec((TM_FFN, D_ATTN), row),
            _const_spec((D_CONV, D_MODEL)),
            _const_spec((D_ATTN, D_MODEL)),
        ]
    args += [gain.reshape(1, D_MODEL), w_gate.astype(BF16), w_up.astype(BF16), w_down.astype(BF16)]
    in_specs += [
        _const_spec((1, D_MODEL)),
        _const_spec((D_MODEL, D_FF)),
        _const_spec((D_MODEL, D_FF)),
        _const_spec((D_FF, D_MODEL)),
    ]
    return pl.pallas_call(
        functools.partial(_ffn_kernel, mixer_out=mixer_out is not None),
        grid=(n_tok // TM_FFN,),
        in_specs=in_specs,
        out_specs=pl.BlockSpec((TM_FFN, D_MODEL), row),
        out_shape=jax.ShapeDtypeStruct((n_tok, D_MODEL), F32),
        scratch_shapes=[pltpu.VMEM((TM_FFN, D_FF), BF16)],
        compiler_params=pltpu.CompilerParams(
            dimension_semantics=("arbitrary",), vmem_limit_bytes=VMEM_LIMIT),
        name="ffn_mix" if mixer_out is not None else "ffn",
    )(*args)


def _split3_bf16(x):
    hi = x.astype(BF16)
    r1 = x - hi.astype(F32)
    mid = r1.astype(BF16)
    lo = (r1 - mid.astype(F32)).astype(BF16)
    return hi, mid, lo


def _even_in_kernel(x_ref, g_ref, w_ref, bf_ref, cw_ref, cb_ref, cn_ref, qg_ref, kg_ref, grp_ref,
                    tri_ref, sel_ref, a_ref, qt_ref, kx_ref, vx_ref, abuf, fcarry, sbuf):
    i = pl.program_id(1)
    tm = x_ref.shape[1]
    x = x_ref[0]
    h = (x * _rms_scale(x) * g_ref[...]).astype(BF16)

    def proj(lo, width):
        return jnp.dot(h, w_ref[:, lo:lo + width], preferred_element_type=F32)

    u = proj(0, D_CONV)
    gate = proj(D_CONV, D_CONV)

    @pl.when(i == 0)
    def _():
        abuf[0:HALO_A, :] = jnp.zeros((HALO_A, D_CONV), F32)
        fcarry[...] = jnp.zeros_like(fcarry)

    @pl.when(i > 0)
    def _():
        abuf[0:HALO_A, :] = abuf[tm:tm + HALO_A, :]

    abuf[HALO_A:HALO_A + tm, :] = u * jax.nn.sigmoid(gate)
    for shift in range(1, SUBLANES):
        sbuf[shift - 1] = abuf[shift:shift + tm + HALO_A - SUBLANES, :]
    base = HALO_A - (CONV_A_WIDTH - 1)
    conv = jnp.zeros((tm, D_CONV), F32) + cb_ref[...]
    for t in range(CONV_A_WIDTH):
        shift = (base + t) % SUBLANES
        lo = base + t - shift
        rows = sbuf[shift - 1, lo:lo + tm, :] if shift else abuf[lo:lo + tm, :]
        conv = conv + cw_ref[t:t + 1, :] * rows
    an = conv * _rms_scale(conv) * cn_ref[...]
    a_ref[0] = (an * jax.nn.sigmoid(an)).astype(BF16)

    def head_norm(z, gain_ref):
        ss = jnp.dot((z * z).astype(BF16), grp_ref[...], preferred_element_type=F32)
        return z * lax.rsqrt(ss * (1.0 / HEAD_DIM) + EPS) * gain_ref[...]

    qt_ref[0] = head_norm(proj(2 * D_CONV, D_ATTN), qg_ref).T.astype(BF16)
    kn = head_norm(proj(2 * D_CONV + D_ATTN, D_ATTN), kg_ref).astype(BF16)
    vt = proj(2 * D_CONV + 2 * D_ATTN, D_ATTN).T
    ones = jnp.ones((HEAD_DIM, tm), BF16)
    for hd in range(N_HEADS):
        vx_ref[0, hd, 0, 0:HEAD_DIM, :] = vt[hd * HEAD_DIM:(hd + 1) * HEAD_DIM, :].astype(BF16)
        vx_ref[0, hd, 0, HEAD_DIM:2 * HEAD_DIM, :] = ones

    fl = proj(2 * D_CONV + 3 * D_ATTN, F_PAD) + bf_ref[...]
    logf = jnp.minimum(fl, 0.0) - jnp.log1p(jnp.exp(-jnp.abs(fl)))
    hi, mid, lo = _split3_bf16(logf)
    tri = tri_ref[...]
    cum = (jnp.dot(tri, hi, preferred_element_type=F32)
           + jnp.dot(tri, mid, preferred_element_type=F32)
           + jnp.dot(tri, lo, preferred_element_type=F32)) + fcarry[...]
    fcarry[...] = cum[tm - 1:tm, :]
    pieces = jnp.concatenate(_split3_bf16(cum * LOG2E), axis=1)
    ext = jnp.dot(pieces, sel_ref[...], preferred_element_type=F32).astype(BF16)
    for p in range(N_HEADS // 2):
        kx_ref[0, :, p * KX_W:p * KX_W + LANES] = kn[:, p * LANES:(p + 1) * LANES]
        kx_ref[0, :, p * KX_W + LANES:(p + 1) * KX_W] = ext[:, p * LANES:(p + 1) * LANES]


def _decay_selector():
    sel = np.zeros((3 * F_PAD, (N_HEADS // 2) * LANES), np.float32)
    for piece in range(3):
        for hd in range(N_HEADS):
            sel[piece * F_PAD + hd, (hd // 2) * LANES + 3 * (hd % 2) + piece] = 1.0
    return jnp.asarray(sel, BF16)


def _even_in(x, gain, w_in, b_f, conv_w, conv_b, conv_norm, q_norm, k_norm):
    bsz, seq, _ = x.shape
    tm = BK
    n_pairs = N_HEADS // 2
    scale = LOG2E / math.sqrt(HEAD_DIM)
    w_pad = jnp.pad(w_in, ((0, 0), (0, F_PAD - N_HEADS))).astype(BF16)
    bf_pad = jnp.pad(b_f, (0, F_PAD - N_HEADS)).reshape(1, F_PAD)
    cw_pad = jnp.pad(conv_w, ((0, 1), (0, 0)))
    qg = (jnp.tile(q_norm, N_HEADS) * scale).reshape(1, D_ATTN)
    kg = jnp.tile(k_norm, N_HEADS).reshape(1, D_ATTN)
    head_of = jnp.arange(D_ATTN) // HEAD_DIM
    grp = (head_of[:, None] == head_of[None, :]).astype(BF16)
    tri = (jnp.arange(tm)[:, None] >= jnp.arange(tm)[None, :]).astype(BF16)
    tok = lambda b, i: (b, i, 0)
    return pl.pallas_call(
        _even_in_kernel,
        grid=(bsz, seq // tm),
        in_specs=[
            pl.BlockSpec((1, tm, D_MODEL), tok),
            _const_spec((1, D_MODEL)),
            _const_spec((D_MODEL, D_IN_EVEN_PAD)),
            _const_spec((1, F_PAD)),
            _const_spec((CONV_A_WIDTH + 1, D_CONV)),
            _const_spec((1, D_CONV)),
            _const_spec((1, D_CONV)),
            _const_spec((1, D_ATTN)),
            _const_spec((1, D_ATTN)),
            _const_spec((D_ATTN, D_ATTN)),
            _const_spec((tm, tm)),
            _const_spec((3 * F_PAD, n_pairs * LANES)),
        ],
        out_specs=[
            pl.BlockSpec((1, tm, D_CONV), tok),
            pl.BlockSpec((1, D_ATTN, tm), lambda b, i: (b, 0, i)),
            pl.BlockSpec((1, tm, n_pairs * KX_W), tok),
            pl.BlockSpec((1, N_HEADS, 1, 2 * HEAD_DIM, tm), lambda b, i: (b, 0, i, 0, 0)),
        ],
        out_shape=[
            jax.ShapeDtypeStruct((bsz, seq, D_CONV), BF16),
            jax.ShapeDtypeStruct((bsz, D_ATTN, seq), BF16),
            jax.ShapeDtypeStruct((bsz, seq, n_pairs * KX_W), BF16),
            jax.ShapeDtypeStruct((bsz, N_HEADS, seq // tm, 2 * HEAD_DIM, tm), BF16),
        ],
        scratch_shapes=[pltpu.VMEM((tm + HALO_A, D_CONV), F32), pltpu.VMEM((1, F_PAD), F32),
                        pltpu.VMEM((SUBLANES - 1, tm + HALO_A - SUBLANES, D_CONV), F32)],
        compiler_params=pltpu.CompilerParams(
            dimension_semantics=("arbitrary", "arbitrary"), vmem_limit_bytes=VMEM_LIMIT),
        name="even_in",
    )(x, gain.reshape(1, D_MODEL), w_pad, bf_pad, cw_pad, conv_b.reshape(1, D_CONV),
      conv_norm.reshape(1, D_CONV), qg, kg, grp, tri, _decay_selector())


def _attn_kernel(qt_ref, kx_ref, vx_ref, o_ref, acc_ref):
    seq = kx_ref.shape[1]
    nq = seq // BQ
    zeros = jnp.zeros((HEAD_DIM, BQ), BF16)
    r = lax.broadcasted_iota(jnp.int32, (LANES, BQ), 0)
    ext = [jnp.where((r >= 3 * hd) & (r < 3 * hd + 3), -1.0, 0.0).astype(BF16) for hd in range(2)]
    key = lax.broadcasted_iota(jnp.int32, (KH, KH), 0)
    qry = lax.broadcasted_iota(jnp.int32, (KH, KH), 1)
    causal = key <= qry
    rhs_cache = {}

    def rhs(i, hd):
        if (i, hd) not in rhs_cache:
            qt = qt_ref[0, :, i * BQ:(i + 1) * BQ]
            top = [qt[0:HEAD_DIM], zeros] if hd == 0 else [zeros, qt[HEAD_DIM:]]
            rhs_cache[(i, hd)] = jnp.concatenate(top + [ext[hd]], axis=0)
        return rhs_cache[(i, hd)]

    def logits(i, j, kh, hd):
        kx = kx_ref[0, j * BK + kh * KH:j * BK + (kh + 1) * KH, :]
        if j < i:
            return jnp.dot(kx, rhs(i, hd), preferred_element_type=F32)
        if kh == 0:
            st = jnp.dot(kx, rhs(i, hd), preferred_element_type=F32)
            return jnp.concatenate([jnp.where(causal, st[:, :KH], -jnp.inf), st[:, KH:]], axis=1)
        st = jnp.dot(kx, rhs(i, hd)[:, KH:], preferred_element_type=F32)
        return jnp.where(causal, st, -jnp.inf)

    units = [(i, j, kh, hd) for i in range(nq) for j in range(i + 1) for kh in range(2)
             for hd in range(2)]
    m = [None, None]
    sts = {k: logits(*units[k]) for k in range(min(PIPE_AHEAD, len(units)))}
    for k, (i, j, kh, hd) in enumerate(units):
        if k + PIPE_AHEAD < len(units):
            sts[k + PIPE_AHEAD] = logits(*units[k + PIPE_AHEAD])
        st = sts.pop(k)
        slot = i % 2
        first = j == 0 and kh == 0
        right_only = j == i and kh == 1
        cols = slice(KH, BQ) if right_only else slice(0, BQ)
        col_max = jnp.max(st, axis=0, keepdims=True)
        if first:
            m_new = col_max
        else:
            m_prev = m[hd][:, cols]
            m_new = jnp.maximum(m_prev, col_max)
            alpha = jnp.exp2(m_prev - m_new)
        pt = jnp.exp2(st - m_new).astype(BF16)
        vx = vx_ref[0, hd, j, :, kh * KH:(kh + 1) * KH]
        pv = jnp.dot(vx, pt, preferred_element_type=F32)
        if first:
            acc_ref[slot, hd] = pv
            m[hd] = m_new
        elif right_only:
            acc_ref[slot, hd, :, KH:] = acc_ref[slot, hd, :, KH:] * alpha + pv
            m[hd] = jnp.concatenate([m[hd][:, :KH], m_new], axis=1)
        else:
            acc_ref[slot, hd] = acc_ref[slot, hd] * alpha + pv
            m[hd] = m_new
        if right_only and hd == 1:
            parts = []
            for h2 in range(2):
                acc = acc_ref[slot, h2]
                parts.append(acc[0:HEAD_DIM] * (1.0 / acc[HEAD_DIM:HEAD_DIM + 1]))
            o_ref[0, i * BQ:(i + 1) * BQ, :] = jnp.concatenate(parts, axis=0).T.astype(BF16)


def _attention(qt, kx, vx):
    bsz, _, seq = qt.shape
    n_pairs = N_HEADS // 2
    return pl.pallas_call(
        _attn_kernel,
        grid=(bsz, n_pairs),
        in_specs=[
            pl.BlockSpec((1, 2 * HEAD_DIM, seq), lambda b, p: (b, p, 0)),
            pl.BlockSpec((1, seq, KX_W), lambda b, p: (b, 0, p)),
            pl.BlockSpec((1, 2, seq // BK, 2 * HEAD_DIM, BK), lambda b, p: (b, p, 0, 0, 0)),
        ],
        out_specs=pl.BlockSpec((1, seq, LANES), lambda b, p: (b, 0, p)),
        out_shape=jax.ShapeDtypeStruct((bsz, seq, D_ATTN), BF16),
        scratch_shapes=[pltpu.VMEM((2, 2, 2 * HEAD_DIM, BQ), F32)],
        compiler_params=pltpu.CompilerParams(
            dimension_semantics=("arbitrary", "arbitrary"), vmem_limit_bytes=VMEM_LIMIT),
        name="fox_attention",
    )(qt, kx, vx)


def _odd_kernel(x_ref, g_ref, wi_ref, cw_ref, wo_ref, y_ref, mbuf):
    i = pl.program_id(1)
    tm = x_ref.shape[1]
    x = x_ref[0]
    h = (x * _rms_scale(x) * g_ref[...]).astype(BF16)

    def proj(k):
        return jnp.dot(h, wi_ref[:, k * D_SHORT:(k + 1) * D_SHORT], preferred_element_type=F32)

    @pl.when(i == 0)
    def _():
        mbuf[0:HALO_C, :] = jnp.zeros((HALO_C, D_SHORT), F32)

    @pl.when(i > 0)
    def _():
        mbuf[0:HALO_C, :] = mbuf[tm:tm + HALO_C, :]

    mbuf[HALO_C:HALO_C + tm, :] = proj(1) * proj(2)
    base = HALO_C - (CONV_C_WIDTH - 1)
    conv = cw_ref[0:1, :] * mbuf[base:base + tm, :]
    for t in range(1, CONV_C_WIDTH):
        conv = conv + cw_ref[t:t + 1, :] * mbuf[base + t:base + t + tm, :]
    y = (proj(0) * conv).astype(BF16)
    y_ref[0] = x + jnp.dot(y, wo_ref[...], preferred_element_type=F32)


def _odd_mixer(x, gain, w_in, conv_w, w_out):
    bsz, seq, _ = x.shape
    tm = TM_MIX
    tok = lambda b, i: (b, i, 0)
    return pl.pallas_call(
        _odd_kernel,
        grid=(bsz, seq // tm),
        in_specs=[
            pl.BlockSpec((1, tm, D_MODEL), tok),
            _const_spec((1, D_MODEL)),
            _const_spec((D_MODEL, 3 * D_SHORT)),
            _const_spec((CONV_C_WIDTH + 1, D_SHORT)),
            _const_spec((D_SHORT, D_MODEL)),
        ],
        out_specs=pl.BlockSpec((1, tm, D_MODEL), tok),
        out_shape=jax.ShapeDtypeStruct((bsz, seq, D_MODEL), F32),
        scratch_shapes=[pltpu.VMEM((tm + HALO_C, D_SHORT), F32)],
        compiler_params=pltpu.CompilerParams(
            dimension_semantics=("arbitrary", "arbitrary"), vmem_limit_bytes=VMEM_LIMIT),
        name="odd_mixer",
    )(x, gain.reshape(1, D_MODEL), w_in.astype(BF16), jnp.pad(conv_w, ((0, 1), (0, 0))),
      w_out.astype(BF16))


def kernel(x, ffn1_norm, ffn1_w_gate, ffn1_w_up, ffn1_w_down, mix_norm, ffn2_norm, ffn2_w_gate,
           ffn2_w_up, ffn2_w_down, ev_w_in, ev_b_f, ev_conv_w, ev_conv_b, ev_conv_norm, ev_q_norm,
           ev_k_norm, ev_w_out, od_w_in, od_conv_w, od_w_out):
    bsz, seq, d = x.shape
    n_tok = bsz * seq
    depth = ffn1_norm.shape[0]
    for layer in range(depth):
        x = _ffn(x.reshape(n_tok, d), ffn1_norm[layer], ffn1_w_gate[layer], ffn1_w_up[layer],
                 ffn1_w_down[layer]).reshape(bsz, seq, d)
        i = layer // 2
        mixer_out = None
        if layer % 2 == 0:
            a, qt, kx, vx = _even_in(x, mix_norm[layer], ev_w_in[i], ev_b_f[i], ev_conv_w[i],
                                     ev_conv_b[i], ev_conv_norm[i], ev_q_norm[i], ev_k_norm[i])
            o = _attention(qt, kx, vx)
            mixer_out = (a.reshape(n_tok, D_CONV), o.reshape(n_tok, D_ATTN), ev_w_out[i])
        else:
            x = _odd_mixer(x, mix_norm[layer], od_w_in[i], od_conv_w[i], od_w_out[i])
        x = _ffn(x.reshape(n_tok, d), ffn2_norm[layer], ffn2_w_gate[layer], ffn2_w_up[layer],
                 ffn2_w_down[layer], mixer_out).reshape(bsz, seq, d)
    return x
```

```python
import functools
import math

import jax
import jax.numpy as jnp
import numpy as np
from jax import lax
from jax.experimental import pallas as pl
from jax.experimental.pallas import tpu as pltpu

F32 = jnp.float32
BF16 = jnp.bfloat16

D_MODEL = 1024
D_FF = 2816
FFN_RES = 0.5
D_CONV = 512
CONV_A_WIDTH = 31
N_HEADS = 8
HEAD_DIM = 64
D_ATTN = N_HEADS * HEAD_DIM
CONV_C_WIDTH = 3
D_SHORT = 1024
EPS = 1e-6

LANES = 128
SUBLANES = 8
F_PAD = LANES
D_IN_EVEN = 2 * D_CONV + 3 * D_ATTN + N_HEADS

TM_FFN = 512
FF_CHUNK = 256
TM_MIX = 512
HALO_A = 32
HALO_C = 8
BQ = 512
KH = 256
PIPE_AHEAD = 3
LOG2E = 1.4426950408889634
BK = 512
KX_W = 2 * LANES
V_ROWS = 96
VMEM_LIMIT = 56 * 1024 * 1024


def _const_spec(shape):
    nd = len(shape)
    return pl.BlockSpec(shape, lambda *_: (0,) * nd, pipeline_mode=pl.Buffered(1))


def _layer_spec(stacked, layer):
    _, rows, cols = stacked.shape
    return pl.BlockSpec((None, rows, cols), lambda *_: (layer, 0, 0), pipeline_mode=pl.Buffered(1))


def _rms_scale(x):
    return lax.rsqrt(jnp.mean(x * x, axis=-1, keepdims=True) + EPS)


def _ffn_kernel(*refs, mixer_out):
    if mixer_out:
        x_ref, a_ref, att_ref, wo_ref, g_ref, wg_ref, wu_ref, wd_ref, o_ref, h_ref = refs
        x = (x_ref[...]
             + jnp.dot(a_ref[...], wo_ref[0:D_CONV, :].astype(BF16), preferred_element_type=F32)
             + jnp.dot(att_ref[...], wo_ref[D_CONV:, :].astype(BF16), preferred_element_type=F32))
    else:
        x_ref, g_ref, wg_ref, wu_ref, wd_ref, o_ref, h_ref = refs
        x = x_ref[...]
    n = (x * _rms_scale(x) * g_ref[...]).astype(BF16)
    for c in range(D_FF // FF_CHUNK):
        sl = slice(c * FF_CHUNK, (c + 1) * FF_CHUNK)
        g = jnp.dot(n, wg_ref[:, sl].astype(BF16), preferred_element_type=F32)
        u = jnp.dot(n, wu_ref[:, sl].astype(BF16), preferred_element_type=F32)
        h_ref[:, sl] = (g * jax.nn.sigmoid(g) * u).astype(BF16)
    y = jnp.dot(h_ref[...], wd_ref[...].astype(BF16), preferred_element_type=F32)
    o_ref[...] = x + FFN_RES * y


def _ffn(x2d, gain, w_gate, w_up, w_down, layer, mixer_out=None):
    n_tok = x2d.shape[0]
    row = lambda i: (i, 0)
    args = [x2d]
    in_specs = [pl.BlockSpec((TM_FFN, D_MODEL), row)]
    if mixer_out is not None:
        a2d, o2d, w_out, idx = mixer_out
        args += [a2d, o2d, w_out]
        in_specs += [
            pl.BlockSpec((TM_FFN, D_CONV), row),
            pl.BlockSpec((TM_FFN, D_ATTN), row),
            _layer_spec(w_out, idx),
        ]
    args += [gain.reshape(1, D_MODEL), w_gate, w_up, w_down]
    in_specs += [
        _const_spec((1, D_MODEL)),
        _layer_spec(w_gate, layer),
        _layer_spec(w_up, layer),
        _layer_spec(w_down, layer),
    ]
    return pl.pallas_call(
        functools.partial(_ffn_kernel, mixer_out=mixer_out is not None),
        grid=(n_tok // TM_FFN,),
        in_specs=in_specs,
        out_specs=pl.BlockSpec((TM_FFN, D_MODEL), row),
        out_shape=jax.ShapeDtypeStruct((n_tok, D_MODEL), F32),
        scratch_shapes=[pltpu.VMEM((TM_FFN, D_FF), BF16)],
        compiler_params=pltpu.CompilerParams(
            dimension_semantics=("arbitrary",), vmem_limit_bytes=VMEM_LIMIT),
        name="ffn_mix" if mixer_out is not None else "ffn",
    )(*args)


def _split3_bf16(x):
    hi = x.astype(BF16)
    r1 = x - hi.astype(F32)
    mid = r1.astype(BF16)
    lo = (r1 - mid.astype(F32)).astype(BF16)
    return hi, mid, lo


def _even_in_kernel(x_ref, g_ref, w_ref, wf_ref, bf_ref, cw_ref, cb_ref, cn_ref, qg_ref, kg_ref,
                    grp_ref, tri_ref, sel_ref, a_ref, qt_ref, kx_ref, vx_ref, abuf, fcarry, sbuf):
    i = pl.program_id(1)
    tm = x_ref.shape[1]
    x = x_ref[0]
    h = (x * _rms_scale(x) * g_ref[...]).astype(BF16)

    def proj(lo, width):
        return jnp.dot(h, w_ref[:, lo:lo + width].astype(BF16), preferred_element_type=F32)

    u = proj(0, D_CONV)
    gate = proj(D_CONV, D_CONV)

    @pl.when(i == 0)
    def _():
        abuf[0:HALO_A, :] = jnp.zeros((HALO_A, D_CONV), F32)
        fcarry[...] = jnp.zeros_like(fcarry)

    @pl.when(i > 0)
    def _():
        abuf[0:HALO_A, :] = abuf[tm:tm + HALO_A, :]

    abuf[HALO_A:HALO_A + tm, :] = u * jax.nn.sigmoid(gate)
    for shift in range(1, SUBLANES):
        sbuf[shift - 1] = abuf[shift:shift + tm + HALO_A - SUBLANES, :]
    base = HALO_A - (CONV_A_WIDTH - 1)
    conv = jnp.zeros((tm, D_CONV), F32) + cb_ref[...]
    for t in range(CONV_A_WIDTH):
        shift = (base + t) % SUBLANES
        lo = base + t - shift
        rows = sbuf[shift - 1, lo:lo + tm, :] if shift else abuf[lo:lo + tm, :]
        conv = conv + cw_ref[t:t + 1, :] * rows
    an = conv * _rms_scale(conv) * cn_ref[...]
    a_ref[0] = (an * jax.nn.sigmoid(an)).astype(BF16)

    def head_norm(z, gain_ref):
        ss = jnp.dot((z * z).astype(BF16), grp_ref[...], preferred_element_type=F32)
        return z * lax.rsqrt(ss * (1.0 / HEAD_DIM) + EPS) * gain_ref[...]

    qt_ref[0] = head_norm(proj(2 * D_CONV, D_ATTN), qg_ref).T.astype(BF16)
    kn = head_norm(proj(2 * D_CONV + D_ATTN, D_ATTN), kg_ref).astype(BF16)
    vt = proj(2 * D_CONV + 2 * D_ATTN, D_ATTN).T
    ones = jnp.ones((V_ROWS - HEAD_DIM, tm), BF16)
    for hd in range(N_HEADS):
        vx_ref[0, hd, 0, 0:HEAD_DIM, :] = vt[hd * HEAD_DIM:(hd + 1) * HEAD_DIM, :].astype(BF16)
        vx_ref[0, hd, 0, HEAD_DIM:, :] = ones

    fl = jnp.dot(h, wf_ref[...].astype(BF16), preferred_element_type=F32) + bf_ref[...]
    logf = jnp.minimum(fl, 0.0) - jnp.log1p(jnp.exp(-jnp.abs(fl)))
    hi, mid, lo = _split3_bf16(logf)
    tri = tri_ref[...]
    cum = (jnp.dot(tri, hi, preferred_element_type=F32)
           + jnp.dot(tri, mid, preferred_element_type=F32)
           + jnp.dot(tri, lo, preferred_element_type=F32)) + fcarry[...]
    fcarry[...] = cum[tm - 1:tm, :]
    pieces = jnp.concatenate(_split3_bf16(cum * LOG2E), axis=1)
    ext = jnp.dot(pieces, sel_ref[...], preferred_element_type=F32).astype(BF16)
    for p in range(N_HEADS // 2):
        kx_ref[0, :, p * KX_W:p * KX_W + LANES] = kn[:, p * LANES:(p + 1) * LANES]
        kx_ref[0, :, p * KX_W + LANES:(p + 1) * KX_W] = ext[:, p * LANES:(p + 1) * LANES]


def _decay_selector():
    sel = np.zeros((3 * F_PAD, (N_HEADS // 2) * LANES), np.float32)
    for piece in range(3):
        for hd in range(N_HEADS):
            sel[piece * F_PAD + hd, (hd // 2) * LANES + 3 * (hd % 2) + piece] = 1.0
    return jnp.asarray(sel, BF16)


def _even_in(x, gain, w_in, idx, b_f, conv_w, conv_b, conv_norm, q_norm, k_norm):
    bsz, seq, _ = x.shape
    tm = BK
    n_pairs = N_HEADS // 2
    scale = LOG2E / math.sqrt(HEAD_DIM)
    w_f = jnp.pad(w_in[idx, :, D_IN_EVEN - N_HEADS:], ((0, 0), (0, F_PAD - N_HEADS)))
    bf_pad = jnp.pad(b_f, (0, F_PAD - N_HEADS)).reshape(1, F_PAD)
    cw_pad = jnp.pad(conv_w, ((0, 1), (0, 0)))
    qg = (jnp.tile(q_norm, N_HEADS) * scale).reshape(1, D_ATTN)
    kg = jnp.tile(k_norm, N_HEADS).reshape(1, D_ATTN)
    head_of = jnp.arange(D_ATTN) // HEAD_DIM
    grp = (head_of[:, None] == head_of[None, :]).astype(BF16)
    tri = (jnp.arange(tm)[:, None] >= jnp.arange(tm)[None, :]).astype(BF16)
    tok = lambda b, i: (b, i, 0)
    return pl.pallas_call(
        _even_in_kernel,
        grid=(bsz, seq // tm),
        in_specs=[
            pl.BlockSpec((1, tm, D_MODEL), tok),
            _const_spec((1, D_MODEL)),
            _layer_spec(w_in, idx),
            _const_spec((D_MODEL, F_PAD)),
            _const_spec((1, F_PAD)),
            _const_spec((CONV_A_WIDTH + 1, D_CONV)),
            _const_spec((1, D_CONV)),
            _const_spec((1, D_CONV)),
            _const_spec((1, D_ATTN)),
            _const_spec((1, D_ATTN)),
            _const_spec((D_ATTN, D_ATTN)),
            _const_spec((tm, tm)),
            _const_spec((3 * F_PAD, n_pairs * LANES)),
        ],
        out_specs=[
            pl.BlockSpec((1, tm, D_CONV), tok),
            pl.BlockSpec((1, D_ATTN, tm), lambda b, i: (b, 0, i)),
            pl.BlockSpec((1, tm, n_pairs * KX_W), tok),
            pl.BlockSpec((1, N_HEADS, 1, V_ROWS, tm), lambda b, i: (b, 0, i, 0, 0)),
        ],
        out_shape=[
            jax.ShapeDtypeStruct((bsz, seq, D_CONV), BF16),
            jax.ShapeDtypeStruct((bsz, D_ATTN, seq), BF16),
            jax.ShapeDtypeStruct((bsz, seq, n_pairs * KX_W), BF16),
            jax.ShapeDtypeStruct((bsz, N_HEADS, seq // tm, V_ROWS, tm), BF16),
        ],
        scratch_shapes=[pltpu.VMEM((tm + HALO_A, D_CONV), F32), pltpu.VMEM((1, F_PAD), F32),
                        pltpu.VMEM((SUBLANES - 1, tm + HALO_A - SUBLANES, D_CONV), F32)],
        compiler_params=pltpu.CompilerParams(
            dimension_semantics=("arbitrary", "arbitrary"), vmem_limit_bytes=VMEM_LIMIT),
        name="even_in",
    )(x, gain.reshape(1, D_MODEL), w_in, w_f, bf_pad, cw_pad, conv_b.reshape(1, D_CONV),
      conv_norm.reshape(1, D_CONV), qg, kg, grp, tri, _decay_selector())


def _attn_kernel(qt_ref, kx_ref, vx_ref, o_ref, acc_ref):
    seq = kx_ref.shape[1]
    nq = seq // BQ
    zeros = jnp.zeros((HEAD_DIM, BQ), BF16)
    r = lax.broadcasted_iota(jnp.int32, (LANES, BQ), 0)
    ext = [jnp.where((r >= 3 * hd) & (r < 3 * hd + 3), -1.0, 0.0).astype(BF16) for hd in range(2)]
    key = lax.broadcasted_iota(jnp.int32, (KH, KH), 0)
    qry = lax.broadcasted_iota(jnp.int32, (KH, KH), 1)
    causal = key <= qry
    rhs_cache = {}

    def rhs(i, hd):
        if (i, hd) not in rhs_cache:
            qt = qt_ref[0, :, i * BQ:(i + 1) * BQ]
            top = [qt[0:HEAD_DIM], zeros] if hd == 0 else [zeros, qt[HEAD_DIM:]]
            rhs_cache[(i, hd)] = jnp.concatenate(top + [ext[hd]], axis=0)
        return rhs_cache[(i, hd)]

    def logits(i, j, kh, hd):
        kx = kx_ref[0, j * BK + kh * KH:j * BK + (kh + 1) * KH, :]
        if j < i:
            return jnp.dot(kx, rhs(i, hd), preferred_element_type=F32)
        if kh == 0:
            st = jnp.dot(kx, rhs(i, hd), preferred_element_type=F32)
            return jnp.concatenate([jnp.where(causal, st[:, :KH], -jnp.inf), st[:, KH:]], axis=1)
        st = jnp.dot(kx, rhs(i, hd)[:, KH:], preferred_element_type=F32)
        return jnp.where(causal, st, -jnp.inf)

    units = [(i, j, kh, hd) for i in range(nq) for j in range(i + 1) for kh in range(2)
             for hd in range(2)]
    m = [None, None]
    sts = {k: logits(*units[k]) for k in range(min(PIPE_AHEAD, len(units)))}
    for k, (i, j, kh, hd) in enumerate(units):
        if k + PIPE_AHEAD < len(units):
            sts[k + PIPE_AHEAD] = logits(*units[k + PIPE_AHEAD])
        st = sts.pop(k)
        slot = i % 2
        first = j == 0 and kh == 0
        right_only = j == i and kh == 1
        cols = slice(KH, BQ) if right_only else slice(0, BQ)
        col_max = jnp.max(st, axis=0, keepdims=True)
        if first:
            m_new = col_max
        else:
            m_prev = m[hd][:, cols]
            m_new = jnp.maximum(m_prev, col_max)
            alpha = jnp.exp2(m_prev - m_new)
        pt = jnp.exp2(st - m_new).astype(BF16)
        vx = vx_ref[0, hd, j, :, kh * KH:(kh + 1) * KH]
        pv = jnp.dot(vx, pt, preferred_element_type=F32)
        if first:
            acc_ref[slot, hd] = pv
            m[hd] = m_new
        elif right_only:
            acc_ref[slot, hd, :, KH:] = acc_ref[slot, hd, :, KH:] * alpha + pv
            m[hd] = jnp.concatenate([m[hd][:, :KH], m_new], axis=1)
        else:
            acc_ref[slot, hd] = acc_ref[slot, hd] * alpha + pv
            m[hd] = m_new
        if right_only and hd == 1:
            parts = []
            for h2 in range(2):
                acc = acc_ref[slot, h2]
                parts.append(acc[0:HEAD_DIM] * (1.0 / acc[HEAD_DIM:HEAD_DIM + 1]))
            o_ref[0, i * BQ:(i + 1) * BQ, :] = jnp.concatenate(parts, axis=0).T.astype(BF16)


def _attention(qt, kx, vx):
    bsz, _, seq = qt.shape
    n_pairs = N_HEADS // 2
    return pl.pallas_call(
        _attn_kernel,
        grid=(bsz, n_pairs),
        in_specs=[
            pl.BlockSpec((1, 2 * HEAD_DIM, seq), lambda b, p: (b, p, 0)),
            pl.BlockSpec((1, seq, KX_W), lambda b, p: (b, 0, p)),
            pl.BlockSpec((1, 2, seq // BK, V_ROWS, BK), lambda b, p: (b, p, 0, 0, 0)),
        ],
        out_specs=pl.BlockSpec((1, seq, LANES), lambda b, p: (b, 0, p)),
        out_shape=jax.ShapeDtypeStruct((bsz, seq, D_ATTN), BF16),
        scratch_shapes=[pltpu.VMEM((2, 2, V_ROWS, BQ), F32)],
        compiler_params=pltpu.CompilerParams(
            dimension_semantics=("arbitrary", "arbitrary"), vmem_limit_bytes=VMEM_LIMIT),
        name="fox_attention",
    )(qt, kx, vx)


def _odd_kernel(x_ref, g_ref, wi_ref, cw_ref, wo_ref, y_ref, mbuf):
    i = pl.program_id(1)
    tm = x_ref.shape[1]
    x = x_ref[0]
    h = (x * _rms_scale(x) * g_ref[...]).astype(BF16)

    def proj(k):
        return jnp.dot(h, wi_ref[:, k * D_SHORT:(k + 1) * D_SHORT].astype(BF16),
                       preferred_element_type=F32)

    @pl.when(i == 0)
    def _():
        mbuf[0:HALO_C, :] = jnp.zeros((HALO_C, D_SHORT), F32)

    @pl.when(i > 0)
    def _():
        mbuf[0:HALO_C, :] = mbuf[tm:tm + HALO_C, :]

    mbuf[HALO_C:HALO_C + tm, :] = proj(1) * proj(2)
    base = HALO_C - (CONV_C_WIDTH - 1)
    conv = cw_ref[0:1, :] * mbuf[base:base + tm, :]
    for t in range(1, CONV_C_WIDTH):
        conv = conv + cw_ref[t:t + 1, :] * mbuf[base + t:base + t + tm, :]
    y = (proj(0) * conv).astype(BF16)
    y_ref[0] = x + jnp.dot(y, wo_ref[...].astype(BF16), preferred_element_type=F32)


def _odd_mixer(x, gain, w_in, conv_w, w_out, idx):
    bsz, seq, _ = x.shape
    tm = TM_MIX
    tok = lambda b, i: (b, i, 0)
    return pl.pallas_call(
        _odd_kernel,
        grid=(bsz, seq // tm),
        in_specs=[
            pl.BlockSpec((1, tm, D_MODEL), tok),
            _const_spec((1, D_MODEL)),
            _layer_spec(w_in, idx),
            _const_spec((CONV_C_WIDTH + 1, D_SHORT)),
            _layer_spec(w_out, idx),
        ],
        out_specs=pl.BlockSpec((1, tm, D_MODEL), tok),
        out_shape=jax.ShapeDtypeStruct((bsz, seq, D_MODEL), F32),
        scratch_shapes=[pltpu.VMEM((tm + HALO_C, D_SHORT), F32)],
        compiler_params=pltpu.CompilerParams(
            dimension_semantics=("arbitrary", "arbitrary"), vmem_limit_bytes=VMEM_LIMIT),
        name="odd_mixer",
    )(x, gain.reshape(1, D_MODEL), w_in, jnp.pad(conv_w, ((0, 1), (0, 0))), w_out)


def kernel(x, ffn1_norm, ffn1_w_gate, ffn1_w_up, ffn1_w_down, mix_norm, ffn2_norm, ffn2_w_gate,
           ffn2_w_up, ffn2_w_down, ev_w_in, ev_b_f, ev_conv_w, ev_conv_b, ev_conv_norm, ev_q_norm,
           ev_k_norm, ev_w_out, od_w_in, od_conv_w, od_w_out):
    bsz, seq, d = x.shape
    n_tok = bsz * seq
    depth = ffn1_norm.shape[0]
    for layer in range(depth):
        x = _ffn(x.reshape(n_tok, d), ffn1_norm[layer], ffn1_w_gate, ffn1_w_up, ffn1_w_down,
                 layer).reshape(bsz, seq, d)
        i = layer // 2
        mixer_out = None
        if layer % 2 == 0:
            a, qt, kx, vx = _even_in(x, mix_norm[layer], ev_w_in, i, ev_b_f[i], ev_conv_w[i],
                                     ev_conv_b[i], ev_conv_norm[i], ev_q_norm[i], ev_k_norm[i])
            o = _attention(qt, kx, vx)
            mixer_out = (a.reshape(n_tok, D_CONV), o.reshape(n_tok, D_ATTN), ev_w_out, i)
        else:
            x = _odd_mixer(x, mix_norm[layer], od_w_in, od_conv_w[i], od_w_out, i)
        x = _ffn(x.reshape(n_tok, d), ffn2_norm[layer], ffn2_w_gate, ffn2_w_up, ffn2_w_down,
                 layer, mixer_out).reshape(bsz, seq, d)
    return x
```

```python
import functools
import math

import jax
import jax.numpy as jnp
import numpy as np
from jax import lax
from jax.experimental import pallas as pl
from jax.experimental.pallas import tpu as pltpu

F32 = jnp.float32
BF16 = jnp.bfloat16

D_MODEL = 1024
D_FF = 2816
FFN_RES = 0.5
D_CONV = 512
CONV_A_WIDTH = 31
N_HEADS = 8
HEAD_DIM = 64
D_ATTN = N_HEADS * HEAD_DIM
CONV_C_WIDTH = 3
D_SHORT = 1024
EPS = 1e-6

LANES = 128
SUBLANES = 8
F_PAD = LANES
D_IN_EVEN = 2 * D_CONV + 3 * D_ATTN + N_HEADS

TM_FFN = 512
FF_CHUNK = 256
TM_MIX = 512
HALO_A = 32
CONV_ROWS = 64
CONV_CH = 256
HALO_C = 8
BQ = 512
KH = 256
PIPE_AHEAD = 3
LOG2E = 1.4426950408889634
BK = 512
KX_W = 2 * LANES
V_ROWS = 96
VMEM_LIMIT = 56 * 1024 * 1024


def _const_spec(shape):
    nd = len(shape)
    return pl.BlockSpec(shape, lambda *_: (0,) * nd, pipeline_mode=pl.Buffered(1))


def _layer_spec(stacked, layer):
    _, rows, cols = stacked.shape
    return pl.BlockSpec((None, rows, cols), lambda *_: (layer, 0, 0), pipeline_mode=pl.Buffered(1))


def _rms_scale(x):
    return lax.rsqrt(jnp.mean(x * x, axis=-1, keepdims=True) + EPS)


def _ffn_kernel(*refs, mixer_out):
    if mixer_out:
        x_ref, a_ref, att_ref, wo_ref, g_ref, wg_ref, wu_ref, wd_ref, o_ref, h_ref = refs
        x = (x_ref[...]
             + jnp.dot(a_ref[...], wo_ref[0:D_CONV, :].astype(BF16), preferred_element_type=F32)
             + jnp.dot(att_ref[...], wo_ref[D_CONV:, :].astype(BF16), preferred_element_type=F32))
    else:
        x_ref, g_ref, wg_ref, wu_ref, wd_ref, o_ref, h_ref = refs
        x = x_ref[...]
    n = (x * _rms_scale(x) * g_ref[...]).astype(BF16)
    for c in range(D_FF // FF_CHUNK):
        sl = slice(c * FF_CHUNK, (c + 1) * FF_CHUNK)
        g = jnp.dot(n, wg_ref[:, sl].astype(BF16), preferred_element_type=F32)
        u = jnp.dot(n, wu_ref[:, sl].astype(BF16), preferred_element_type=F32)
        h_ref[:, sl] = (g * jax.nn.sigmoid(g) * u).astype(BF16)
    y = jnp.dot(h_ref[...], wd_ref[...].astype(BF16), preferred_element_type=F32)
    o_ref[...] = x + FFN_RES * y


def _ffn(x2d, gain, w_gate, w_up, w_down, layer, mixer_out=None):
    n_tok = x2d.shape[0]
    row = lambda i: (i, 0)
    args = [x2d]
    in_specs = [pl.BlockSpec((TM_FFN, D_MODEL), row)]
    if mixer_out is not None:
        a2d, o2d, w_out, idx = mixer_out
        args += [a2d, o2d, w_out]
        in_specs += [
            pl.BlockSpec((TM_FFN, D_CONV), row),
            pl.BlockSpec((TM_FFN, D_ATTN), row),
            _layer_spec(w_out, idx),
        ]
    args += [gain.reshape(1, D_MODEL), w_gate, w_up, w_down]
    in_specs += [
        _const_spec((1, D_MODEL)),
        _layer_spec(w_gate, layer),
        _layer_spec(w_up, layer),
        _layer_spec(w_down, layer),
    ]
    return pl.pallas_call(
        functools.partial(_ffn_kernel, mixer_out=mixer_out is not None),
        grid=(n_tok // TM_FFN,),
        in_specs=in_specs,
        out_specs=pl.BlockSpec((TM_FFN, D_MODEL), row),
        out_shape=jax.ShapeDtypeStruct((n_tok, D_MODEL), F32),
        scratch_shapes=[pltpu.VMEM((TM_FFN, D_FF), BF16)],
        compiler_params=pltpu.CompilerParams(
            dimension_semantics=("arbitrary",), vmem_limit_bytes=VMEM_LIMIT),
        name="ffn_mix" if mixer_out is not None else "ffn",
    )(*args)


def _split3_bf16(x):
    hi = x.astype(BF16)
    r1 = x - hi.astype(F32)
    mid = r1.astype(BF16)
    lo = (r1 - mid.astype(F32)).astype(BF16)
    return hi, mid, lo


def _even_in_kernel(x_ref, g_ref, w_ref, wf_ref, bf_ref, cw_ref, cb_ref, cn_ref, qg_ref, kg_ref,
                    grp_ref, tri_ref, sel_ref, a_ref, qt_ref, kx_ref, vx_ref, abuf, fcarry, sbuf,
                    cbuf):
    i = pl.program_id(1)
    tm = x_ref.shape[1]
    x = x_ref[0]
    h = (x * _rms_scale(x) * g_ref[...]).astype(BF16)

    def proj(lo, width):
        return jnp.dot(h, w_ref[:, lo:lo + width].astype(BF16), preferred_element_type=F32)

    @pl.when(i == 0)
    def _():
        abuf[0:HALO_A, :] = jnp.zeros((HALO_A, D_CONV), F32)
        fcarry[...] = jnp.zeros_like(fcarry)

    @pl.when(i > 0)
    def _():
        abuf[0:HALO_A, :] = abuf[tm:tm + HALO_A, :]

    base = HALO_A - (CONV_A_WIDTH - 1)

    def glu(cb):
        cs = slice(cb * CONV_CH, (cb + 1) * CONV_CH)
        u = proj(cb * CONV_CH, CONV_CH)
        gate = proj(D_CONV + cb * CONV_CH, CONV_CH)
        abuf[HALO_A:HALO_A + tm, cs] = u * jax.nn.sigmoid(gate)

    def conv_block(cb):
        cs = slice(cb * CONV_CH, (cb + 1) * CONV_CH)
        for shift in range(1, SUBLANES):
            sbuf[shift - 1, :, cs] = abuf[shift:shift + tm + HALO_A - SUBLANES, cs]
        for r0 in range(0, tm, CONV_ROWS):
            conv = jnp.zeros((CONV_ROWS, CONV_CH), F32) + cb_ref[:, cs]
            for t in range(CONV_A_WIDTH):
                shift = (base + t) % SUBLANES
                lo = base + t - shift + r0
                rows = (sbuf[shift - 1, lo:lo + CONV_ROWS, cs] if shift
                        else abuf[lo:lo + CONV_ROWS, cs])
                conv = conv + cw_ref[t:t + 1, cs] * rows
            cbuf[r0:r0 + CONV_ROWS, cs] = conv

    def head_norm(z, gain_ref):
        ss = jnp.dot((z * z).astype(BF16), grp_ref[...], preferred_element_type=F32)
        return z * lax.rsqrt(ss * (1.0 / HEAD_DIM) + EPS) * gain_ref[...]

    def q_part():
        qt_ref[0] = head_norm(proj(2 * D_CONV, D_ATTN), qg_ref).T.astype(BF16)

    def k_part():
        kn = head_norm(proj(2 * D_CONV + D_ATTN, D_ATTN), kg_ref).astype(BF16)
        for p in range(N_HEADS // 2):
            kx_ref[0, :, p * KX_W:p * KX_W + LANES] = kn[:, p * LANES:(p + 1) * LANES]

    def v_part():
        vt = proj(2 * D_CONV + 2 * D_ATTN, D_ATTN).T
        ones = jnp.ones((V_ROWS - HEAD_DIM, tm), BF16)
        for hd in range(N_HEADS):
            vx_ref[0, hd, 0, 0:HEAD_DIM, :] = vt[hd * HEAD_DIM:(hd + 1) * HEAD_DIM, :].astype(BF16)
            vx_ref[0, hd, 0, HEAD_DIM:, :] = ones

    def decay_part():
        fl = jnp.dot(h, wf_ref[...].astype(BF16), preferred_element_type=F32) + bf_ref[...]
        logf = jnp.minimum(fl, 0.0) - jnp.log1p(jnp.exp(-jnp.abs(fl)))
        hi, mid, lo = _split3_bf16(logf)
        tri = tri_ref[...]
        cum = (jnp.dot(tri, hi, preferred_element_type=F32)
               + jnp.dot(tri, mid, preferred_element_type=F32)
               + jnp.dot(tri, lo, preferred_element_type=F32)) + fcarry[...]
        fcarry[...] = cum[tm - 1:tm, :]
        pieces = jnp.concatenate(_split3_bf16(cum * LOG2E), axis=1)
        ext = jnp.dot(pieces, sel_ref[...], preferred_element_type=F32).astype(BF16)
        for p in range(N_HEADS // 2):
            kx_ref[0, :, p * KX_W + LANES:(p + 1) * KX_W] = ext[:, p * LANES:(p + 1) * LANES]

    glu(0)
    glu(1)
    conv_block(0)
    q_part()
    k_part()
    v_part()
    decay_part()
    conv_block(1)
    conv = cbuf[...]
    ss = jnp.dot((conv * conv).astype(BF16), jnp.ones((D_CONV, LANES), BF16),
                 preferred_element_type=F32)[:, 0:1]
    an = conv * lax.rsqrt(ss * (1.0 / D_CONV) + EPS) * cn_ref[...]
    a_ref[0] = (an * jax.nn.sigmoid(an)).astype(BF16)


def _decay_selector():
    sel = np.zeros((3 * F_PAD, (N_HEADS // 2) * LANES), np.float32)
    for piece in range(3):
        for hd in range(N_HEADS):
            sel[piece * F_PAD + hd, (hd // 2) * LANES + 3 * (hd % 2) + piece] = 1.0
    return jnp.asarray(sel, BF16)


def _even_in(x, gain, w_in, idx, b_f, conv_w, conv_b, conv_norm, q_norm, k_norm):
    bsz, seq, _ = x.shape
    tm = BK
    n_pairs = N_HEADS // 2
    scale = LOG2E / math.sqrt(HEAD_DIM)
    w_f = jnp.pad(w_in[idx, :, D_IN_EVEN - N_HEADS:], ((0, 0), (0, F_PAD - N_HEADS)))
    bf_pad = jnp.pad(b_f, (0, F_PAD - N_HEADS)).reshape(1, F_PAD)
    cw_pad = jnp.pad(conv_w, ((0, 1), (0, 0)))
    qg = (jnp.tile(q_norm, N_HEADS) * scale).reshape(1, D_ATTN)
    kg = jnp.tile(k_norm, N_HEADS).reshape(1, D_ATTN)
    head_of = jnp.arange(D_ATTN) // HEAD_DIM
    grp = (head_of[:, None] == head_of[None, :]).astype(BF16)
    tri = (jnp.arange(tm)[:, None] >= jnp.arange(tm)[None, :]).astype(BF16)
    tok = lambda b, i: (b, i, 0)
    return pl.pallas_call(
        _even_in_kernel,
        grid=(bsz, seq // tm),
        in_specs=[
            pl.BlockSpec((1, tm, D_MODEL), tok),
            _const_spec((1, D_MODEL)),
            _layer_spec(w_in, idx),
            _const_spec((D_MODEL, F_PAD)),
            _const_spec((1, F_PAD)),
            _const_spec((CONV_A_WIDTH + 1, D_CONV)),
            _const_spec((1, D_CONV)),
            _const_spec((1, D_CONV)),
            _const_spec((1, D_ATTN)),
            _const_spec((1, D_ATTN)),
            _const_spec((D_ATTN, D_ATTN)),
            _const_spec((tm, tm)),
            _const_spec((3 * F_PAD, n_pairs * LANES)),
        ],
        out_specs=[
            pl.BlockSpec((1, tm, D_CONV), tok),
            pl.BlockSpec((1, D_ATTN, tm), lambda b, i: (b, 0, i)),
            pl.BlockSpec((1, tm, n_pairs * KX_W), tok),
            pl.BlockSpec((1, N_HEADS, 1, V_ROWS, tm), lambda b, i: (b, 0, i, 0, 0)),
        ],
        out_shape=[
            jax.ShapeDtypeStruct((bsz, seq, D_CONV), BF16),
            jax.ShapeDtypeStruct((bsz, D_ATTN, seq), BF16),
            jax.ShapeDtypeStruct((bsz, seq, n_pairs * KX_W), BF16),
            jax.ShapeDtypeStruct((bsz, N_HEADS, seq // tm, V_ROWS, tm), BF16),
        ],
        scratch_shapes=[pltpu.VMEM((tm + HALO_A, D_CONV), F32), pltpu.VMEM((1, F_PAD), F32),
                        pltpu.VMEM((SUBLANES - 1, tm + HALO_A - SUBLANES, D_CONV), F32),
                        pltpu.VMEM((tm, D_CONV), F32)],
        compiler_params=pltpu.CompilerParams(
            dimension_semantics=("arbitrary", "arbitrary"), vmem_limit_bytes=VMEM_LIMIT),
        name="even_in",
    )(x, gain.reshape(1, D_MODEL), w_in, w_f, bf_pad, cw_pad, conv_b.reshape(1, D_CONV),
      conv_norm.reshape(1, D_CONV), qg, kg, grp, tri, _decay_selector())


def _attn_kernel(qt_ref, kx_ref, vx_ref, o_ref, acc_ref):
    seq = kx_ref.shape[1]
    nq = seq // BQ
    zeros = jnp.zeros((HEAD_DIM, BQ), BF16)
    r = lax.broadcasted_iota(jnp.int32, (LANES, BQ), 0)
    ext = [jnp.where((r >= 3 * hd) & (r < 3 * hd + 3), -1.0, 0.0).astype(BF16) for hd in range(2)]
    key = lax.broadcasted_iota(jnp.int32, (KH, KH), 0)
    qry = lax.broadcasted_iota(jnp.int32, (KH, KH), 1)
    causal = key <= qry
    rhs_cache = {}

    def rhs(i, hd):
        if (i, hd) not in rhs_cache:
            qt = qt_ref[0, :, i * BQ:(i + 1) * BQ]
            top = [qt[0:HEAD_DIM], zeros] if hd == 0 else [zeros, qt[HEAD_DIM:]]
            rhs_cache[(i, hd)] = jnp.concatenate(top + [ext[hd]], axis=0)
        return rhs_cache[(i, hd)]

    def logits(i, j, kh, hd):
        kx = kx_ref[0, j * BK + kh * KH:j * BK + (kh + 1) * KH, :]
        if j < i:
            return jnp.dot(kx, rhs(i, hd), preferred_element_type=F32)
        if kh == 0:
            st = jnp.dot(kx, rhs(i, hd), preferred_element_type=F32)
            return jnp.concatenate([jnp.where(causal, st[:, :KH], -jnp.inf), st[:, KH:]], axis=1)
        st = jnp.dot(kx, rhs(i, hd)[:, KH:], preferred_element_type=F32)
        return jnp.where(causal, st, -jnp.inf)

    units = [(i, j, kh, hd) for i in range(nq) for j in range(i + 1) for kh in range(2)
             for hd in range(2)]
    m = [None, None]
    sts = {k: logits(*units[k]) for k in range(min(PIPE_AHEAD, len(units)))}
    for k, (i, j, kh, hd) in enumerate(units):
        if k + PIPE_AHEAD < len(units):
            sts[k + PIPE_AHEAD] = logits(*units[k + PIPE_AHEAD])
        st = sts.pop(k)
        slot = i % 2
        first = j == 0 and kh == 0
        right_only = j == i and kh == 1
        cols = slice(KH, BQ) if right_only else slice(0, BQ)
        col_max = jnp.max(st, axis=0, keepdims=True)
        if first:
            m_new = col_max
        else:
            m_prev = m[hd][:, cols]
            m_new = jnp.maximum(m_prev, col_max)
            alpha = jnp.exp2(m_prev - m_new)
        pt = jnp.exp2(st - m_new).astype(BF16)
        vx = vx_ref[0, hd, j, :, kh * KH:(kh + 1) * KH]
        pv = jnp.dot(vx, pt, preferred_element_type=F32)
        if first:
            acc_ref[slot, hd] = pv
            m[hd] = m_new
        elif right_only:
            acc_ref[slot, hd, :, KH:] = acc_ref[slot, hd, :, KH:] * alpha + pv
            m[hd] = jnp.concatenate([m[hd][:, :KH], m_new], axis=1)
        else:
            acc_ref[slot, hd] = acc_ref[slot, hd] * alpha + pv
            m[hd] = m_new
        if right_only and hd == 1:
            parts = []
            for h2 in range(2):
                acc = acc_ref[slot, h2]
                parts.append(acc[0:HEAD_DIM] * (1.0 / acc[HEAD_DIM:HEAD_DIM + 1]))
            o_ref[0, i * BQ:(i + 1) * BQ, :] = jnp.concatenate(parts, axis=0).T.astype(BF16)


def _attention(qt, kx, vx):
    bsz, _, seq = qt.shape
    n_pairs = N_HEADS // 2
    return pl.pallas_call(
        _attn_kernel,
        grid=(bsz, n_pairs),
        in_specs=[
            pl.BlockSpec((1, 2 * HEAD_DIM, seq), lambda b, p: (b, p, 0)),
            pl.BlockSpec((1, seq, KX_W), lambda b, p: (b, 0, p)),
            pl.BlockSpec((1, 2, seq // BK, V_ROWS, BK), lambda b, p: (b, p, 0, 0, 0)),
        ],
        out_specs=pl.BlockSpec((1, seq, LANES), lambda b, p: (b, 0, p)),
        out_shape=jax.ShapeDtypeStruct((bsz, seq, D_ATTN), BF16),
        scratch_shapes=[pltpu.VMEM((2, 2, V_ROWS, BQ), F32)],
        compiler_params=pltpu.CompilerParams(
            dimension_semantics=("arbitrary", "arbitrary"), vmem_limit_bytes=VMEM_LIMIT),
        name="fox_attention",
    )(qt, kx, vx)


def _odd_kernel(x_ref, g_ref, wi_ref, cw_ref, wo_ref, y_ref, mbuf):
    i = pl.program_id(1)
    tm = x_ref.shape[1]
    x = x_ref[0]
    h = (x * _rms_scale(x) * g_ref[...]).astype(BF16)

    def proj(k):
        return jnp.dot(h, wi_ref[:, k * D_SHORT:(k + 1) * D_SHORT].astype(BF16),
                       preferred_element_type=F32)

    @pl.when(i == 0)
    def _():
        mbuf[0:HALO_C, :] = jnp.zeros((HALO_C, D_SHORT), F32)

    @pl.when(i > 0)
    def _():
        mbuf[0:HALO_C, :] = mbuf[tm:tm + HALO_C, :]

    mbuf[HALO_C:HALO_C + tm, :] = proj(1) * proj(2)
    base = HALO_C - (CONV_C_WIDTH - 1)
    conv = cw_ref[0:1, :] * mbuf[base:base + tm, :]
    for t in range(1, CONV_C_WIDTH):
        conv = conv + cw_ref[t:t + 1, :] * mbuf[base + t:base + t + tm, :]
    y = (proj(0) * conv).astype(BF16)
    y_ref[0] = x + jnp.dot(y, wo_ref[...].astype(BF16), preferred_element_type=F32)


def _odd_mixer(x, gain, w_in, conv_w, w_out, idx):
    bsz, seq, _ = x.shape
    tm = TM_MIX
    tok = lambda b, i: (b, i, 0)
    return pl.pallas_call(
        _odd_kernel,
        grid=(bsz, seq // tm),
        in_specs=[
            pl.BlockSpec((1, tm, D_MODEL), tok),
            _const_spec((1, D_MODEL)),
            _layer_spec(w_in, idx),
            _const_spec((CONV_C_WIDTH + 1, D_SHORT)),
            _layer_spec(w_out, idx),
        ],
        out_specs=pl.BlockSpec((1, tm, D_MODEL), tok),
        out_shape=jax.ShapeDtypeStruct((bsz, seq, D_MODEL), F32),
        scratch_shapes=[pltpu.VMEM((tm + HALO_C, D_SHORT), F32)],
        compiler_params=pltpu.CompilerParams(
            dimension_semantics=("arbitrary", "arbitrary"), vmem_limit_bytes=VMEM_LIMIT),
        name="odd_mixer",
    )(x, gain.reshape(1, D_MODEL), w_in, jnp.pad(conv_w, ((0, 1), (0, 0))), w_out)


def kernel(x, ffn1_norm, ffn1_w_gate, ffn1_w_up, ffn1_w_down, mix_norm, ffn2_norm, ffn2_w_gate,
           ffn2_w_up, ffn2_w_down, ev_w_in, ev_b_f, ev_conv_w, ev_conv_b, ev_conv_norm, ev_q_norm,
           ev_k_norm, ev_w_out, od_w_in, od_conv_w, od_w_out):
    bsz, seq, d = x.shape
    n_tok = bsz * seq
    depth = ffn1_norm.shape[0]
    for layer in range(depth):
        x = _ffn(x.reshape(n_tok, d), ffn1_norm[layer], ffn1_w_gate, ffn1_w_up, ffn1_w_down,
                 layer).reshape(bsz, seq, d)
        i = layer // 2
        mixer_out = None
        if layer % 2 == 0:
            a, qt, kx, vx = _even_in(x, mix_norm[layer], ev_w_in, i, ev_b_f[i], ev_conv_w[i],
                                     ev_conv_b[i], ev_conv_norm[i], ev_q_norm[i], ev_k_norm[i])
            o = _attention(qt, kx, vx)
            mixer_out = (a.reshape(n_tok, D_CONV), o.reshape(n_tok, D_ATTN), ev_w_out, i)
        else:
            x = _odd_mixer(x, mix_norm[layer], od_w_in, od_conv_w[i], od_w_out, i)
        x = _ffn(x.reshape(n_tok, d), ffn2_norm[layer], ffn2_w_gate, ffn2_w_up, ffn2_w_down,
                 layer, mixer_out).reshape(bsz, seq, d)
    return x
```

```python
import functools
import math

import jax
import jax.numpy as jnp
import numpy as np
from jax import lax
from jax.experimental import pallas as pl
from jax.experimental.pallas import tpu as pltpu

F32 = jnp.float32
BF16 = jnp.bfloat16

D_MODEL = 1024
D_FF = 2816
FFN_RES = 0.5
D_CONV = 512
CONV_A_WIDTH = 31
N_HEADS = 8
HEAD_DIM = 64
D_ATTN = N_HEADS * HEAD_DIM
CONV_C_WIDTH = 3
D_SHORT = 1024
EPS = 1e-6

LANES = 128
SUBLANES = 8
F_PAD = LANES
D_IN_EVEN = 2 * D_CONV + 3 * D_ATTN + N_HEADS

TM_FFN = 512
FF_CHUNK = 256
TM_MIX = 512
HALO_A = 32
CONV_ROWS = 64
CONV_CH = 256
HALO_C = 8
BQ = 512
KH = 256
PIPE_AHEAD = 2
LOG2E = 1.4426950408889634
BK = 512
KX_W = 2 * LANES
V_ROWS = 96
VMEM_LIMIT = 56 * 1024 * 1024


def _const_spec(shape):
    nd = len(shape)
    return pl.BlockSpec(shape, lambda *_: (0,) * nd, pipeline_mode=pl.Buffered(1))


def _layer_spec(stacked, layer):
    _, rows, cols = stacked.shape
    return pl.BlockSpec((None, rows, cols), lambda *_: (layer, 0, 0), pipeline_mode=pl.Buffered(1))


def _rms_scale(x):
    return lax.rsqrt(jnp.mean(x * x, axis=-1, keepdims=True) + EPS)


def _ffn_kernel(*refs, mixer_out):
    if mixer_out:
        x_ref, a_ref, att_ref, wo_ref, g_ref, wg_ref, wu_ref, wd_ref, o_ref, h_ref = refs
        x = (x_ref[...]
             + jnp.dot(a_ref[...], wo_ref[0:D_CONV, :].astype(BF16), preferred_element_type=F32)
             + jnp.dot(att_ref[...], wo_ref[D_CONV:, :].astype(BF16), preferred_element_type=F32))
    else:
        x_ref, g_ref, wg_ref, wu_ref, wd_ref, o_ref, h_ref = refs
        x = x_ref[...]
    n = (x * _rms_scale(x) * g_ref[...]).astype(BF16)
    for c in range(D_FF // FF_CHUNK):
        sl = slice(c * FF_CHUNK, (c + 1) * FF_CHUNK)
        g = jnp.dot(n, wg_ref[:, sl].astype(BF16), preferred_element_type=F32)
        u = jnp.dot(n, wu_ref[:, sl].astype(BF16), preferred_element_type=F32)
        h_ref[:, sl] = (g * jax.nn.sigmoid(g) * u).astype(BF16)
    y = jnp.dot(h_ref[...], wd_ref[...].astype(BF16), preferred_element_type=F32)
    o_ref[...] = x + FFN_RES * y


def _ffn(x2d, gain, w_gate, w_up, w_down, layer, mixer_out=None):
    n_tok = x2d.shape[0]
    row = lambda i: (i, 0)
    args = [x2d]
    in_specs = [pl.BlockSpec((TM_FFN, D_MODEL), row)]
    if mixer_out is not None:
        a2d, o2d, w_out, idx = mixer_out
        args += [a2d, o2d, w_out]
        in_specs += [
            pl.BlockSpec((TM_FFN, D_CONV), row),
            pl.BlockSpec((TM_FFN, D_ATTN), row),
            _layer_spec(w_out, idx),
        ]
    args += [gain.reshape(1, D_MODEL), w_gate, w_up, w_down]
    in_specs += [
        _const_spec((1, D_MODEL)),
        _layer_spec(w_gate, layer),
        _layer_spec(w_up, layer),
        _layer_spec(w_down, layer),
    ]
    return pl.pallas_call(
        functools.partial(_ffn_kernel, mixer_out=mixer_out is not None),
        grid=(n_tok // TM_FFN,),
        in_specs=in_specs,
        out_specs=pl.BlockSpec((TM_FFN, D_MODEL), row),
        out_shape=jax.ShapeDtypeStruct((n_tok, D_MODEL), F32),
        scratch_shapes=[pltpu.VMEM((TM_FFN, D_FF), BF16)],
        compiler_params=pltpu.CompilerParams(
            dimension_semantics=("arbitrary",), vmem_limit_bytes=VMEM_LIMIT),
        name="ffn_mix" if mixer_out is not None else "ffn",
    )(*args)


def _split3_bf16(x):
    hi = x.astype(BF16)
    r1 = x - hi.astype(F32)
    mid = r1.astype(BF16)
    lo = (r1 - mid.astype(F32)).astype(BF16)
    return hi, mid, lo


def _even_in_kernel(x_ref, g_ref, w_ref, wf_ref, bf_ref, cw_ref, cb_ref, cn_ref, qg_ref, kg_ref,
                    grp_ref, tri_ref, sel_ref, a_ref, qt_ref, kx_ref, vx_ref, abuf, fcarry, sbuf,
                    cbuf):
    i = pl.program_id(1)
    tm = x_ref.shape[1]
    x = x_ref[0]
    h = (x * _rms_scale(x) * g_ref[...]).astype(BF16)

    def proj(lo, width):
        return jnp.dot(h, w_ref[:, lo:lo + width].astype(BF16), preferred_element_type=F32)

    @pl.when(i == 0)
    def _():
        abuf[0:HALO_A, :] = jnp.zeros((HALO_A, D_CONV), F32)
        fcarry[...] = jnp.zeros_like(fcarry)

    @pl.when(i > 0)
    def _():
        abuf[0:HALO_A, :] = abuf[tm:tm + HALO_A, :]

    base = HALO_A - (CONV_A_WIDTH - 1)

    def glu(cb):
        cs = slice(cb * CONV_CH, (cb + 1) * CONV_CH)
        u = proj(cb * CONV_CH, CONV_CH)
        gate = proj(D_CONV + cb * CONV_CH, CONV_CH)
        abuf[HALO_A:HALO_A + tm, cs] = u * jax.nn.sigmoid(gate)

    def conv_block(cb):
        cs = slice(cb * CONV_CH, (cb + 1) * CONV_CH)
        for shift in range(1, SUBLANES):
            sbuf[shift - 1, :, cs] = abuf[shift:shift + tm + HALO_A - SUBLANES, cs]
        for r0 in range(0, tm, CONV_ROWS):
            conv = jnp.zeros((CONV_ROWS, CONV_CH), F32) + cb_ref[:, cs]
            for t in range(CONV_A_WIDTH):
                shift = (base + t) % SUBLANES
                lo = base + t - shift + r0
                rows = (sbuf[shift - 1, lo:lo + CONV_ROWS, cs] if shift
                        else abuf[lo:lo + CONV_ROWS, cs])
                conv = conv + cw_ref[t:t + 1, cs] * rows
            cbuf[r0:r0 + CONV_ROWS, cs] = conv

    def head_norm(z, gain_ref):
        ss = jnp.dot((z * z).astype(BF16), grp_ref[...], preferred_element_type=F32)
        return z * lax.rsqrt(ss * (1.0 / HEAD_DIM) + EPS) * gain_ref[...]

    def q_part():
        qt_ref[0] = head_norm(proj(2 * D_CONV, D_ATTN), qg_ref).T.astype(BF16)

    def k_part():
        kn = head_norm(proj(2 * D_CONV + D_ATTN, D_ATTN), kg_ref).astype(BF16)
        for p in range(N_HEADS // 2):
            kx_ref[0, :, p * KX_W:p * KX_W + LANES] = kn[:, p * LANES:(p + 1) * LANES]

    def v_part():
        vt = proj(2 * D_CONV + 2 * D_ATTN, D_ATTN).T
        ones = jnp.ones((V_ROWS - HEAD_DIM, tm), BF16)
        for hd in range(N_HEADS):
            vx_ref[0, hd, 0, 0:HEAD_DIM, :] = vt[hd * HEAD_DIM:(hd + 1) * HEAD_DIM, :].astype(BF16)
            vx_ref[0, hd, 0, HEAD_DIM:, :] = ones

    def decay_part():
        fl = jnp.dot(h, wf_ref[...].astype(BF16), preferred_element_type=F32) + bf_ref[...]
        logf = jnp.minimum(fl, 0.0) - jnp.log1p(jnp.exp(-jnp.abs(fl)))
        hi, mid, lo = _split3_bf16(logf)
        tri = tri_ref[...]
        cum = (jnp.dot(tri, hi, preferred_element_type=F32)
               + jnp.dot(tri, mid, preferred_element_type=F32)
               + jnp.dot(tri, lo, preferred_element_type=F32)) + fcarry[...]
        fcarry[...] = cum[tm - 1:tm, :]
        pieces = jnp.concatenate(_split3_bf16(cum * LOG2E), axis=1)
        ext = jnp.dot(pieces, sel_ref[...], preferred_element_type=F32).astype(BF16)
        for p in range(N_HEADS // 2):
            kx_ref[0, :, p * KX_W + LANES:(p + 1) * KX_W] = ext[:, p * LANES:(p + 1) * LANES]

    glu(0)
    glu(1)
    conv_block(0)
    q_part()
    k_part()
    v_part()
    decay_part()
    conv_block(1)
    conv = cbuf[...]
    ss = jnp.dot((conv * conv).astype(BF16), jnp.ones((D_CONV, LANES), BF16),
                 preferred_element_type=F32)[:, 0:1]
    an = conv * lax.rsqrt(ss * (1.0 / D_CONV) + EPS) * cn_ref[...]
    a_ref[0] = (an * jax.nn.sigmoid(an)).astype(BF16)


def _decay_selector():
    sel = np.zeros((3 * F_PAD, (N_HEADS // 2) * LANES), np.float32)
    for piece in range(3):
        for hd in range(N_HEADS):
            sel[piece * F_PAD + hd, (hd // 2) * LANES + 3 * (hd % 2) + piece] = 1.0
    return jnp.asarray(sel, BF16)


def _even_in(x, gain, w_in, idx, b_f, conv_w, conv_b, conv_norm, q_norm, k_norm):
    bsz, seq, _ = x.shape
    tm = BK
    n_pairs = N_HEADS // 2
    scale = LOG2E / math.sqrt(HEAD_DIM)
    w_f = jnp.pad(w_in[idx, :, D_IN_EVEN - N_HEADS:], ((0, 0), (0, F_PAD - N_HEADS)))
    bf_pad = jnp.pad(b_f, (0, F_PAD - N_HEADS)).reshape(1, F_PAD)
    cw_pad = jnp.pad(conv_w, ((0, 1), (0, 0)))
    qg = (jnp.tile(q_norm, N_HEADS) * scale).reshape(1, D_ATTN)
    kg = jnp.tile(k_norm, N_HEADS).reshape(1, D_ATTN)
    head_of = jnp.arange(D_ATTN) // HEAD_DIM
    grp = (head_of[:, None] == head_of[None, :]).astype(BF16)
    tri = (jnp.arange(tm)[:, None] >= jnp.arange(tm)[None, :]).astype(BF16)
    tok = lambda b, i: (b, i, 0)
    return pl.pallas_call(
        _even_in_kernel,
        grid=(bsz, seq // tm),
        in_specs=[
            pl.BlockSpec((1, tm, D_MODEL), tok),
            _const_spec((1, D_MODEL)),
            _layer_spec(w_in, idx),
            _const_spec((D_MODEL, F_PAD)),
            _const_spec((1, F_PAD)),
            _const_spec((CONV_A_WIDTH + 1, D_CONV)),
            _const_spec((1, D_CONV)),
            _const_spec((1, D_CONV)),
            _const_spec((1, D_ATTN)),
            _const_spec((1, D_ATTN)),
            _const_spec((D_ATTN, D_ATTN)),
            _const_spec((tm, tm)),
            _const_spec((3 * F_PAD, n_pairs * LANES)),
        ],
        out_specs=[
            pl.BlockSpec((1, tm, D_CONV), tok),
            pl.BlockSpec((1, D_ATTN, tm), lambda b, i: (b, 0, i)),
            pl.BlockSpec((1, tm, n_pairs * KX_W), tok),
            pl.BlockSpec((1, N_HEADS, 1, V_ROWS, tm), lambda b, i: (b, 0, i, 0, 0)),
        ],
        out_shape=[
            jax.ShapeDtypeStruct((bsz, seq, D_CONV), BF16),
            jax.ShapeDtypeStruct((bsz, D_ATTN, seq), BF16),
            jax.ShapeDtypeStruct((bsz, seq, n_pairs * KX_W), BF16),
            jax.ShapeDtypeStruct((bsz, N_HEADS, seq // tm, V_ROWS, tm), BF16),
        ],
        scratch_shapes=[pltpu.VMEM((tm + HALO_A, D_CONV), F32), pltpu.VMEM((1, F_PAD), F32),
                        pltpu.VMEM((SUBLANES - 1, tm + HALO_A - SUBLANES, D_CONV), F32),
                        pltpu.VMEM((tm, D_CONV), F32)],
        compiler_params=pltpu.CompilerParams(
            dimension_semantics=("arbitrary", "arbitrary"), vmem_limit_bytes=VMEM_LIMIT),
        name="even_in",
    )(x, gain.reshape(1, D_MODEL), w_in, w_f, bf_pad, cw_pad, conv_b.reshape(1, D_CONV),
      conv_norm.reshape(1, D_CONV), qg, kg, grp, tri, _decay_selector())


def _attn_kernel(qt_ref, kx_ref, vx_ref, o_ref, acc_ref):
    seq = kx_ref.shape[1]
    nq = seq // BQ
    zeros = jnp.zeros((HEAD_DIM, BQ), BF16)
    r = lax.broadcasted_iota(jnp.int32, (LANES, BQ), 0)
    ext = [jnp.where((r >= 3 * hd) & (r < 3 * hd + 3), -1.0, 0.0).astype(BF16) for hd in range(2)]
    key = lax.broadcasted_iota(jnp.int32, (KH, KH), 0)
    qry = lax.broadcasted_iota(jnp.int32, (KH, KH), 1)
    causal = key <= qry
    rhs_cache = {}

    def rhs(i, hd):
        if (i, hd) not in rhs_cache:
            qt = qt_ref[0, :, i * BQ:(i + 1) * BQ]
            top = [qt[0:HEAD_DIM], zeros] if hd == 0 else [zeros, qt[HEAD_DIM:]]
            rhs_cache[(i, hd)] = jnp.concatenate(top + [ext[hd]], axis=0)
        return rhs_cache[(i, hd)]

    def logits(i, j, kh, hd):
        kx = kx_ref[0, j * BK + kh * KH:j * BK + (kh + 1) * KH, :]
        if j < i:
            return jnp.dot(kx, rhs(i, hd), preferred_element_type=F32)
        if kh == 0:
            st = jnp.dot(kx, rhs(i, hd), preferred_element_type=F32)
            return jnp.concatenate([jnp.where(causal, st[:, :KH], -jnp.inf), st[:, KH:]], axis=1)
        st = jnp.dot(kx, rhs(i, hd)[:, KH:], preferred_element_type=F32)
        return jnp.where(causal, st, -jnp.inf)

    units = [(i, j, kh, hd) for i in range(nq) for j in range(i + 1) for kh in range(2)
             for hd in range(2)]
    m = [None, None]
    sts = {k: logits(*units[k]) for k in range(min(PIPE_AHEAD, len(units)))}
    for k, (i, j, kh, hd) in enumerate(units):
        if k + PIPE_AHEAD < len(units):
            sts[k + PIPE_AHEAD] = logits(*units[k + PIPE_AHEAD])
        st = sts.pop(k)
        slot = i % 2
        first = j == 0 and kh == 0
        right_only = j == i and kh == 1
        cols = slice(KH, BQ) if right_only else slice(0, BQ)
        col_max = jnp.max(st, axis=0, keepdims=True)
        if first:
            m_new = col_max
        else:
            m_prev = m[hd][:, cols]
            m_new = jnp.maximum(m_prev, col_max)
            alpha = jnp.exp2(m_prev - m_new)
        pt = jnp.exp2(st - m_new).astype(BF16)
        vx = vx_ref[0, hd, j, :, kh * KH:(kh + 1) * KH]
        pv = jnp.dot(vx, pt, preferred_element_type=F32)
        if first:
            acc_ref[slot, hd] = pv
            m[hd] = m_new
        elif right_only:
            acc_ref[slot, hd, :, KH:] = acc_ref[slot, hd, :, KH:] * alpha + pv
            m[hd] = jnp.concatenate([m[hd][:, :KH], m_new], axis=1)
        else:
            acc_ref[slot, hd] = acc_ref[slot, hd] * alpha + pv
            m[hd] = m_new
        if right_only and hd == 1:
            parts = []
            for h2 in range(2):
                acc = acc_ref[slot, h2]
                parts.append(acc[0:HEAD_DIM] * (1.0 / acc[HEAD_DIM:HEAD_DIM + 1]))
            o_ref[0, i * BQ:(i + 1) * BQ, :] = jnp.concatenate(parts, axis=0).T.astype(BF16)


def _attention(qt, kx, vx):
    bsz, _, seq = qt.shape
    n_pairs = N_HEADS // 2
    return pl.pallas_call(
        _attn_kernel,
        grid=(bsz, n_pairs),
        in_specs=[
            pl.BlockSpec((1, 2 * HEAD_DIM, seq), lambda b, p: (b, p, 0)),
            pl.BlockSpec((1, seq, KX_W), lambda b, p: (b, 0, p)),
            pl.BlockSpec((1, 2, seq // BK, V_ROWS, BK), lambda b, p: (b, p, 0, 0, 0)),
        ],
        out_specs=pl.BlockSpec((1, seq, LANES), lambda b, p: (b, 0, p)),
        out_shape=jax.ShapeDtypeStruct((bsz, seq, D_ATTN), BF16),
        scratch_shapes=[pltpu.VMEM((2, 2, V_ROWS, BQ), F32)],
        compiler_params=pltpu.CompilerParams(
            dimension_semantics=("arbitrary", "arbitrary"), vmem_limit_bytes=VMEM_LIMIT),
        name="fox_attention",
    )(qt, kx, vx)


def _odd_kernel(x_ref, g_ref, wi_ref, cw_ref, wo_ref, y_ref, mbuf):
    i = pl.program_id(1)
    tm = x_ref.shape[1]
    x = x_ref[0]
    h = (x * _rms_scale(x) * g_ref[...]).astype(BF16)

    def proj(k):
        return jnp.dot(h, wi_ref[:, k * D_SHORT:(k + 1) * D_SHORT].astype(BF16),
                       preferred_element_type=F32)

    @pl.when(i == 0)
    def _():
        mbuf[0:HALO_C, :] = jnp.zeros((HALO_C, D_SHORT), F32)

    @pl.when(i > 0)
    def _():
        mbuf[0:HALO_C, :] = mbuf[tm:tm + HALO_C, :]

    mbuf[HALO_C:HALO_C + tm, :] = proj(1) * proj(2)
    base = HALO_C - (CONV_C_WIDTH - 1)
    conv = cw_ref[0:1, :] * mbuf[base:base + tm, :]
    for t in range(1, CONV_C_WIDTH):
        conv = conv + cw_ref[t:t + 1, :] * mbuf[base + t:base + t + tm, :]
    y = (proj(0) * conv).astype(BF16)
    y_ref[0] = x + jnp.dot(y, wo_ref[...].astype(BF16), preferred_element_type=F32)


def _odd_mixer(x, gain, w_in, conv_w, w_out, idx):
    bsz, seq, _ = x.shape
    tm = TM_MIX
    tok = lambda b, i: (b, i, 0)
    return pl.pallas_call(
        _odd_kernel,
        grid=(bsz, seq // tm),
        in_specs=[
            pl.BlockSpec((1, tm, D_MODEL), tok),
            _const_spec((1, D_MODEL)),
            _layer_spec(w_in, idx),
            _const_spec((CONV_C_WIDTH + 1, D_SHORT)),
            _layer_spec(w_out, idx),
        ],
        out_specs=pl.BlockSpec((1, tm, D_MODEL), tok),
        out_shape=jax.ShapeDtypeStruct((bsz, seq, D_MODEL), F32),
        scratch_shapes=[pltpu.VMEM((tm + HALO_C, D_SHORT), F32)],
        compiler_params=pltpu.CompilerParams(
            dimension_semantics=("arbitrary", "arbitrary"), vmem_limit_bytes=VMEM_LIMIT),
        name="odd_mixer",
    )(x, gain.reshape(1, D_MODEL), w_in, jnp.pad(conv_w, ((0, 1), (0, 0))), w_out)


def kernel(x, ffn1_norm, ffn1_w_gate, ffn1_w_up, ffn1_w_down, mix_norm, ffn2_norm, ffn2_w_gate,
           ffn2_w_up, ffn2_w_down, ev_w_in, ev_b_f, ev_conv_w, ev_conv_b, ev_conv_norm, ev_q_norm,
           ev_k_norm, ev_w_out, od_w_in, od_conv_w, od_w_out):
    bsz, seq, d = x.shape
    n_tok = bsz * seq
    depth = ffn1_norm.shape[0]
    for layer in range(depth):
        x = _ffn(x.reshape(n_tok, d), ffn1_norm[layer], ffn1_w_gate, ffn1_w_up, ffn1_w_down,
                 layer).reshape(bsz, seq, d)
        i = layer // 2
        mixer_out = None
        if layer % 2 == 0:
            a, qt, kx, vx = _even_in(x, mix_norm[layer], ev_w_in, i, ev_b_f[i], ev_conv_w[i],
                                     ev_conv_b[i], ev_conv_norm[i], ev_q_norm[i], ev_k_norm[i])
            o = _attention(qt, kx, vx)
            mixer_out = (a.reshape(n_tok, D_CONV), o.reshape(n_tok, D_ATTN), ev_w_out, i)
        else:
            x = _odd_mixer(x, mix_norm[layer], od_w_in, od_conv_w[i], od_w_out, i)
        x = _ffn(x.reshape(n_tok, d), ffn2_norm[layer], ffn2_w_gate, ffn2_w_up, ffn2_w_down,
                 layer, mixer_out).reshape(bsz, seq, d)
    return x
```

```python
import functools
import math

import jax
import jax.numpy as jnp
import numpy as np
from jax import lax
from jax.experimental import pallas as pl
from jax.experimental.pallas import tpu as pltpu

F32 = jnp.float32
BF16 = jnp.bfloat16

D_MODEL = 1024
D_FF = 2816
FFN_RES = 0.5
D_CONV = 512
CONV_A_WIDTH = 31
N_HEADS = 8
HEAD_DIM = 64
D_ATTN = N_HEADS * HEAD_DIM
CONV_C_WIDTH = 3
D_SHORT = 1024
EPS = 1e-6

LANES = 128
SUBLANES = 8
F_PAD = LANES
D_IN_EVEN = 2 * D_CONV + 3 * D_ATTN + N_HEADS

TM_FFN = 512
FF_CHUNK = 256
TM_MIX = 512
HALO_A = 32
CONV_ROWS = 64
CONV_CH = 256
HALO_C = 8
BQ = 512
KH = 256
PIPE_AHEAD = 2
LOG2E = 1.4426950408889634
BK = 512
KX_W = 2 * LANES
V_ROWS = 96
VMEM_LIMIT = 56 * 1024 * 1024


def _const_spec(shape):
    nd = len(shape)
    return pl.BlockSpec(shape, lambda *_: (0,) * nd, pipeline_mode=pl.Buffered(1))


def _layer_spec(stacked, layer):
    _, rows, cols = stacked.shape
    return pl.BlockSpec((None, rows, cols), lambda *_: (layer, 0, 0), pipeline_mode=pl.Buffered(1))


def _rms_scale(x):
    return lax.rsqrt(jnp.mean(x * x, axis=-1, keepdims=True) + EPS)


def _ffn_kernel(*refs, mixer_out):
    if mixer_out:
        x_ref, a_ref, att_ref, wo_ref, g_ref, wg_ref, wu_ref, wd_ref, o_ref, h_ref = refs
        x = (x_ref[...]
             + jnp.dot(a_ref[...], wo_ref[0:D_CONV, :].astype(BF16), preferred_element_type=F32)
             + jnp.dot(att_ref[...], wo_ref[D_CONV:, :].astype(BF16), preferred_element_type=F32))
    else:
        x_ref, g_ref, wg_ref, wu_ref, wd_ref, o_ref, h_ref = refs
        x = x_ref[...]
    xg = (x * g_ref[...]).astype(BF16)
    scale = _rms_scale(x)
    for c in range(D_FF // FF_CHUNK):
        sl = slice(c * FF_CHUNK, (c + 1) * FF_CHUNK)
        g = scale * jnp.dot(xg, wg_ref[:, sl].astype(BF16), preferred_element_type=F32)
        u = scale * jnp.dot(xg, wu_ref[:, sl].astype(BF16), preferred_element_type=F32)
        h_ref[:, sl] = (g * jax.nn.sigmoid(g) * u).astype(BF16)
    y = jnp.dot(h_ref[...], wd_ref[...].astype(BF16), preferred_element_type=F32)
    o_ref[...] = x + FFN_RES * y


def _ffn(x2d, gain, w_gate, w_up, w_down, layer, mixer_out=None):
    n_tok = x2d.shape[0]
    row = lambda i: (i, 0)
    args = [x2d]
    in_specs = [pl.BlockSpec((TM_FFN, D_MODEL), row)]
    if mixer_out is not None:
        a2d, o2d, w_out, idx = mixer_out
        args += [a2d, o2d, w_out]
        in_specs += [
            pl.BlockSpec((TM_FFN, D_CONV), row),
            pl.BlockSpec((TM_FFN, D_ATTN), row),
            _layer_spec(w_out, idx),
        ]
    args += [gain.reshape(1, D_MODEL), w_gate, w_up, w_down]
    in_specs += [
        _const_spec((1, D_MODEL)),
        _layer_spec(w_gate, layer),
        _layer_spec(w_up, layer),
        _layer_spec(w_down, layer),
    ]
    return pl.pallas_call(
        functools.partial(_ffn_kernel, mixer_out=mixer_out is not None),
        grid=(n_tok // TM_FFN,),
        in_specs=in_specs,
        out_specs=pl.BlockSpec((TM_FFN, D_MODEL), row),
        out_shape=jax.ShapeDtypeStruct((n_tok, D_MODEL), F32),
        scratch_shapes=[pltpu.VMEM((TM_FFN, D_FF), BF16)],
        compiler_params=pltpu.CompilerParams(
            dimension_semantics=("arbitrary",), vmem_limit_bytes=VMEM_LIMIT),
        name="ffn_mix" if mixer_out is not None else "ffn",
    )(*args)


def _split3_bf16(x):
    hi = x.astype(BF16)
    r1 = x - hi.astype(F32)
    mid = r1.astype(BF16)
    lo = (r1 - mid.astype(F32)).astype(BF16)
    return hi, mid, lo


def _even_in_kernel(x_ref, g_ref, w_ref, wf_ref, bf_ref, cw_ref, cb_ref, cn_ref, qg_ref, kg_ref,
                    grp_ref, tri_ref, sel_ref, a_ref, qt_ref, kx_ref, vx_ref, abuf, fcarry, sbuf,
                    cbuf):
    i = pl.program_id(1)
    tm = x_ref.shape[1]
    x = x_ref[0]
    h = (x * _rms_scale(x) * g_ref[...]).astype(BF16)

    def proj(lo, width):
        return jnp.dot(h, w_ref[:, lo:lo + width].astype(BF16), preferred_element_type=F32)

    @pl.when(i == 0)
    def _():
        abuf[0:HALO_A, :] = jnp.zeros((HALO_A, D_CONV), F32)
        fcarry[...] = jnp.zeros_like(fcarry)

    @pl.when(i > 0)
    def _():
        abuf[0:HALO_A, :] = abuf[tm:tm + HALO_A, :]

    base = HALO_A - (CONV_A_WIDTH - 1)

    def glu(cb):
        cs = slice(cb * CONV_CH, (cb + 1) * CONV_CH)
        u = proj(cb * CONV_CH, CONV_CH)
        gate = proj(D_CONV + cb * CONV_CH, CONV_CH)
        abuf[HALO_A:HALO_A + tm, cs] = u * jax.nn.sigmoid(gate)

    def conv_block(cb):
        cs = slice(cb * CONV_CH, (cb + 1) * CONV_CH)
        for shift in range(1, SUBLANES):
            sbuf[shift - 1, :, cs] = abuf[shift:shift + tm + HALO_A - SUBLANES, cs]
        for r0 in range(0, tm, CONV_ROWS):
            conv = jnp.zeros((CONV_ROWS, CONV_CH), F32) + cb_ref[:, cs]
            for t in range(CONV_A_WIDTH):
                shift = (base + t) % SUBLANES
                lo = base + t - shift + r0
                rows = (sbuf[shift - 1, lo:lo + CONV_ROWS, cs] if shift
                        else abuf[lo:lo + CONV_ROWS, cs])
                conv = conv + cw_ref[t:t + 1, cs] * rows
            cbuf[r0:r0 + CONV_ROWS, cs] = conv

    def head_norm(z, gain_ref):
        ss = jnp.dot((z * z).astype(BF16), grp_ref[...], preferred_element_type=F32)
        return z * lax.rsqrt(ss * (1.0 / HEAD_DIM) + EPS) * gain_ref[...]

    def q_part():
        qt_ref[0] = head_norm(proj(2 * D_CONV, D_ATTN), qg_ref).T.astype(BF16)

    def k_part():
        kn = head_norm(proj(2 * D_CONV + D_ATTN, D_ATTN), kg_ref).astype(BF16)
        for p in range(N_HEADS // 2):
            kx_ref[0, :, p * KX_W:p * KX_W + LANES] = kn[:, p * LANES:(p + 1) * LANES]

    def v_part():
        vt = proj(2 * D_CONV + 2 * D_ATTN, D_ATTN).T
        ones = jnp.ones((V_ROWS - HEAD_DIM, tm), BF16)
        for hd in range(N_HEADS):
            vx_ref[0, hd, 0, 0:HEAD_DIM, :] = vt[hd * HEAD_DIM:(hd + 1) * HEAD_DIM, :].astype(BF16)
            vx_ref[0, hd, 0, HEAD_DIM:, :] = ones

    def decay_part():
        fl = jnp.dot(h, wf_ref[...].astype(BF16), preferred_element_type=F32) + bf_ref[...]
        logf = jnp.minimum(fl, 0.0) - jnp.log1p(jnp.exp(-jnp.abs(fl)))
        hi, mid, lo = _split3_bf16(logf)
        tri = tri_ref[...]
        cum = (jnp.dot(tri, hi, preferred_element_type=F32)
               + jnp.dot(tri, mid, preferred_element_type=F32)
               + jnp.dot(tri, lo, preferred_element_type=F32)) + fcarry[...]
        fcarry[...] = cum[tm - 1:tm, :]
        pieces = jnp.concatenate(_split3_bf16(cum * LOG2E), axis=1)
        ext = jnp.dot(pieces, sel_ref[...], preferred_element_type=F32).astype(BF16)
        for p in range(N_HEADS // 2):
            kx_ref[0, :, p * KX_W + LANES:(p + 1) * KX_W] = ext[:, p * LANES:(p + 1) * LANES]

    glu(0)
    glu(1)
    conv_block(0)
    q_part()
    k_part()
    v_part()
    decay_part()
    conv_block(1)
    conv = cbuf[...]
    ss = jnp.dot((conv * conv).astype(BF16), jnp.ones((D_CONV, LANES), BF16),
                 preferred_element_type=F32)[:, 0:1]
    an = conv * lax.rsqrt(ss * (1.0 / D_CONV) + EPS) * cn_ref[...]
    a_ref[0] = (an * jax.nn.sigmoid(an)).astype(BF16)


def _decay_selector():
    sel = np.zeros((3 * F_PAD, (N_HEADS // 2) * LANES), np.float32)
    for piece in range(3):
        for hd in range(N_HEADS):
            sel[piece * F_PAD + hd, (hd // 2) * LANES + 3 * (hd % 2) + piece] = 1.0
    return jnp.asarray(sel, BF16)


def _even_in(x, gain, w_in, idx, b_f, conv_w, conv_b, conv_norm, q_norm, k_norm):
    bsz, seq, _ = x.shape
    tm = BK
    n_pairs = N_HEADS // 2
    scale = LOG2E / math.sqrt(HEAD_DIM)
    w_f = jnp.pad(w_in[idx, :, D_IN_EVEN - N_HEADS:], ((0, 0), (0, F_PAD - N_HEADS)))
    bf_pad = jnp.pad(b_f, (0, F_PAD - N_HEADS)).reshape(1, F_PAD)
    cw_pad = jnp.pad(conv_w, ((0, 1), (0, 0)))
    qg = (jnp.tile(q_norm, N_HEADS) * scale).reshape(1, D_ATTN)
    kg = jnp.tile(k_norm, N_HEADS).reshape(1, D_ATTN)
    head_of = jnp.arange(D_ATTN) // HEAD_DIM
    grp = (head_of[:, None] == head_of[None, :]).astype(BF16)
    tri = (jnp.arange(tm)[:, None] >= jnp.arange(tm)[None, :]).astype(BF16)
    tok = lambda b, i: (b, i, 0)
    return pl.pallas_call(
        _even_in_kernel,
        grid=(bsz, seq // tm),
        in_specs=[
            pl.BlockSpec((1, tm, D_MODEL), tok),
            _const_spec((1, D_MODEL)),
            _layer_spec(w_in, idx),
            _const_spec((D_MODEL, F_PAD)),
            _const_spec((1, F_PAD)),
            _const_spec((CONV_A_WIDTH + 1, D_CONV)),
            _const_spec((1, D_CONV)),
            _const_spec((1, D_CONV)),
            _const_spec((1, D_ATTN)),
            _const_spec((1, D_ATTN)),
            _const_spec((D_ATTN, D_ATTN)),
            _const_spec((tm, tm)),
            _const_spec((3 * F_PAD, n_pairs * LANES)),
        ],
        out_specs=[
            pl.BlockSpec((1, tm, D_CONV), tok),
            pl.BlockSpec((1, D_ATTN, tm), lambda b, i: (b, 0, i)),
            pl.BlockSpec((1, tm, n_pairs * KX_W), tok),
            pl.BlockSpec((1, N_HEADS, 1, V_ROWS, tm), lambda b, i: (b, 0, i, 0, 0)),
        ],
        out_shape=[
            jax.ShapeDtypeStruct((bsz, seq, D_CONV), BF16),
            jax.ShapeDtypeStruct((bsz, D_ATTN, seq), BF16),
            jax.ShapeDtypeStruct((bsz, seq, n_pairs * KX_W), BF16),
            jax.ShapeDtypeStruct((bsz, N_HEADS, seq // tm, V_ROWS, tm), BF16),
        ],
        scratch_shapes=[pltpu.VMEM((tm + HALO_A, D_CONV), F32), pltpu.VMEM((1, F_PAD), F32),
                        pltpu.VMEM((SUBLANES - 1, tm + HALO_A - SUBLANES, D_CONV), F32),
                        pltpu.VMEM((tm, D_CONV), F32)],
        compiler_params=pltpu.CompilerParams(
            dimension_semantics=("arbitrary", "arbitrary"), vmem_limit_bytes=VMEM_LIMIT),
        name="even_in",
    )(x, gain.reshape(1, D_MODEL), w_in, w_f, bf_pad, cw_pad, conv_b.reshape(1, D_CONV),
      conv_norm.reshape(1, D_CONV), qg, kg, grp, tri, _decay_selector())


def _attn_kernel(qt_ref, kx_ref, vx_ref, o_ref, acc_ref):
    seq = kx_ref.shape[1]
    nq = seq // BQ
    zeros = jnp.zeros((HEAD_DIM, BQ), BF16)
    r = lax.broadcasted_iota(jnp.int32, (LANES, BQ), 0)
    ext = [jnp.where((r >= 3 * hd) & (r < 3 * hd + 3), -1.0, 0.0).astype(BF16) for hd in range(2)]
    key = lax.broadcasted_iota(jnp.int32, (KH, KH), 0)
    qry = lax.broadcasted_iota(jnp.int32, (KH, KH), 1)
    causal = key <= qry
    rhs_cache = {}

    def rhs(i, hd):
        if (i, hd) not in rhs_cache:
            qt = qt_ref[0, :, i * BQ:(i + 1) * BQ]
            top = [qt[0:HEAD_DIM], zeros] if hd == 0 else [zeros, qt[HEAD_DIM:]]
            rhs_cache[(i, hd)] = jnp.concatenate(top + [ext[hd]], axis=0)
        return rhs_cache[(i, hd)]

    def logits(i, j, kh, hd):
        kx = kx_ref[0, j * BK + kh * KH:j * BK + (kh + 1) * KH, :]
        if j < i:
            return jnp.dot(kx, rhs(i, hd), preferred_element_type=F32)
        if kh == 0:
            st = jnp.dot(kx, rhs(i, hd), preferred_element_type=F32)
            return jnp.concatenate([jnp.where(causal, st[:, :KH], -jnp.inf), st[:, KH:]], axis=1)
        st = jnp.dot(kx, rhs(i, hd)[:, KH:], preferred_element_type=F32)
        return jnp.where(causal, st, -jnp.inf)

    units = [(i, j, kh, hd) for i in range(nq) for j in range(i + 1) for kh in range(2)
             for hd in range(2)]
    m = [None, None]
    sts = {k: logits(*units[k]) for k in range(min(PIPE_AHEAD, len(units)))}
    for k, (i, j, kh, hd) in enumerate(units):
        if k + PIPE_AHEAD < len(units):
            sts[k + PIPE_AHEAD] = logits(*units[k + PIPE_AHEAD])
        st = sts.pop(k)
        slot = i % 2
        first = j == 0 and kh == 0
        right_only = j == i and kh == 1
        cols = slice(KH, BQ) if right_only else slice(0, BQ)
        col_max = jnp.max(st, axis=0, keepdims=True)
        if first:
            m_new = col_max
        else:
            m_prev = m[hd][:, cols]
            m_new = jnp.maximum(m_prev, col_max)
            alpha = jnp.exp2(m_prev - m_new)
        pt = jnp.exp2(st - m_new).astype(BF16)
        vx = vx_ref[0, hd, j, :, kh * KH:(kh + 1) * KH]
        pv = jnp.dot(vx, pt, preferred_element_type=F32)
        if first:
            acc_ref[slot, hd] = pv
            m[hd] = m_new
        elif right_only:
            acc_ref[slot, hd, :, KH:] = acc_ref[slot, hd, :, KH:] * alpha + pv
            m[hd] = jnp.concatenate([m[hd][:, :KH], m_new], axis=1)
        else:
            acc_ref[slot, hd] = acc_ref[slot, hd] * alpha + pv
            m[hd] = m_new
        if right_only and hd == 1:
            parts = []
            for h2 in range(2):
                acc = acc_ref[slot, h2]
                parts.append(acc[0:HEAD_DIM] * (1.0 / acc[HEAD_DIM:HEAD_DIM + 1]))
            o_ref[0, i * BQ:(i + 1) * BQ, :] = jnp.concatenate(parts, axis=0).T.astype(BF16)


def _attention(qt, kx, vx):
    bsz, _, seq = qt.shape
    n_pairs = N_HEADS // 2
    return pl.pallas_call(
        _attn_kernel,
        grid=(bsz, n_pairs),
        in_specs=[
            pl.BlockSpec((1, 2 * HEAD_DIM, seq), lambda b, p: (b, p, 0)),
            pl.BlockSpec((1, seq, KX_W), lambda b, p: (b, 0, p)),
            pl.BlockSpec((1, 2, seq // BK, V_ROWS, BK), lambda b, p: (b, p, 0, 0, 0)),
        ],
        out_specs=pl.BlockSpec((1, seq, LANES), lambda b, p: (b, 0, p)),
        out_shape=jax.ShapeDtypeStruct((bsz, seq, D_ATTN), BF16),
        scratch_shapes=[pltpu.VMEM((2, 2, V_ROWS, BQ), F32)],
        compiler_params=pltpu.CompilerParams(
            dimension_semantics=("arbitrary", "arbitrary"), vmem_limit_bytes=VMEM_LIMIT),
        name="fox_attention",
    )(qt, kx, vx)


def _odd_kernel(x_ref, g_ref, wi_ref, cw_ref, wo_ref, y_ref, mbuf):
    i = pl.program_id(1)
    tm = x_ref.shape[1]
    x = x_ref[0]
    h = (x * _rms_scale(x) * g_ref[...]).astype(BF16)

    def proj(k):
        return jnp.dot(h, wi_ref[:, k * D_SHORT:(k + 1) * D_SHORT].astype(BF16),
                       preferred_element_type=F32)

    @pl.when(i == 0)
    def _():
        mbuf[0:HALO_C, :] = jnp.zeros((HALO_C, D_SHORT), F32)

    @pl.when(i > 0)
    def _():
        mbuf[0:HALO_C, :] = mbuf[tm:tm + HALO_C, :]

    mbuf[HALO_C:HALO_C + tm, :] = proj(1) * proj(2)
    base = HALO_C - (CONV_C_WIDTH - 1)
    conv = cw_ref[0:1, :] * mbuf[base:base + tm, :]
    for t in range(1, CONV_C_WIDTH):
        conv = conv + cw_ref[t:t + 1, :] * mbuf[base + t:base + t + tm, :]
    y = (proj(0) * conv).astype(BF16)
    y_ref[0] = x + jnp.dot(y, wo_ref[...].astype(BF16), preferred_element_type=F32)


def _odd_mixer(x, gain, w_in, conv_w, w_out, idx):
    bsz, seq, _ = x.shape
    tm = TM_MIX
    tok = lambda b, i: (b, i, 0)
    return pl.pallas_call(
        _odd_kernel,
        grid=(bsz, seq // tm),
        in_specs=[
            pl.BlockSpec((1, tm, D_MODEL), tok),
            _const_spec((1, D_MODEL)),
            _layer_spec(w_in, idx),
            _const_spec((CONV_C_WIDTH + 1, D_SHORT)),
            _layer_spec(w_out, idx),
        ],
        out_specs=pl.BlockSpec((1, tm, D_MODEL), tok),
        out_shape=jax.ShapeDtypeStruct((bsz, seq, D_MODEL), F32),
        scratch_shapes=[pltpu.VMEM((tm + HALO_C, D_SHORT), F32)],
        compiler_params=pltpu.CompilerParams(
            dimension_semantics=("arbitrary", "arbitrary"), vmem_limit_bytes=VMEM_LIMIT),
        name="odd_mixer",
    )(x, gain.reshape(1, D_MODEL), w_in, jnp.pad(conv_w, ((0, 1), (0, 0))), w_out)


def kernel(x, ffn1_norm, ffn1_w_gate, ffn1_w_up, ffn1_w_down, mix_norm, ffn2_norm, ffn2_w_gate,
           ffn2_w_up, ffn2_w_down, ev_w_in, ev_b_f, ev_conv_w, ev_conv_b, ev_conv_norm, ev_q_norm,
           ev_k_norm, ev_w_out, od_w_in, od_conv_w, od_w_out):
    bsz, seq, d = x.shape
    n_tok = bsz * seq
    depth = ffn1_norm.shape[0]
    for layer in range(depth):
        x = _ffn(x.reshape(n_tok, d), ffn1_norm[layer], ffn1_w_gate, ffn1_w_up, ffn1_w_down,
                 layer).reshape(bsz, seq, d)
        i = layer // 2
        mixer_out = None
        if layer % 2 == 0:
            a, qt, kx, vx = _even_in(x, mix_norm[layer], ev_w_in, i, ev_b_f[i], ev_conv_w[i],
                                     ev_conv_b[i], ev_conv_norm[i], ev_q_norm[i], ev_k_norm[i])
            o = _attention(qt, kx, vx)
            mixer_out = (a.reshape(n_tok, D_CONV), o.reshape(n_tok, D_ATTN), ev_w_out, i)
        else:
            x = _odd_mixer(x, mix_norm[layer], od_w_in, od_conv_w[i], od_w_out, i)
        x = _ffn(x.reshape(n_tok, d), ffn2_norm[layer], ffn2_w_gate, ffn2_w_up, ffn2_w_down,
                 layer, mixer_out).reshape(bsz, seq, d)
    return x
```

```python
import functools
import math

import jax
import jax.numpy as jnp
import numpy as np
from jax import lax
from jax.experimental import pallas as pl
from jax.experimental.pallas import tpu as pltpu

F32 = jnp.float32
BF16 = jnp.bfloat16

D_MODEL = 1024
D_FF = 2816
FFN_RES = 0.5
D_CONV = 512
CONV_A_WIDTH = 31
N_HEADS = 8
HEAD_DIM = 64
D_ATTN = N_HEADS * HEAD_DIM
CONV_C_WIDTH = 3
D_SHORT = 1024
EPS = 1e-6

LANES = 128
SUBLANES = 8
F_PAD = LANES
D_IN_EVEN = 2 * D_CONV + 3 * D_ATTN + N_HEADS

FFN_ROWS = 512
FF_CHUNK = 256
TM_MIX = 512
HALO_A = 32
CONV_ROWS = 64
CONV_CH = 256
HALO_C = 8
BQ = 512
KH = 256
PIPE_AHEAD = 2
LOG2E = 1.4426950408889634
BK = 512
KX_W = 2 * LANES
V_ROWS = 96
V7X_VMEM_BYTES = 64 * 1024 * 1024
VMEM_LIMIT = V7X_VMEM_BYTES * 7 // 8


def _const_spec(shape):
    nd = len(shape)
    return pl.BlockSpec(shape, lambda *_: (0,) * nd, pipeline_mode=pl.Buffered(1))


def _layer_spec(stacked, layer):
    _, rows, cols = stacked.shape
    return pl.BlockSpec((None, rows, cols), lambda *_: (layer, 0, 0), pipeline_mode=pl.Buffered(1))


def _rms_scale(x):
    return lax.rsqrt(jnp.mean(x * x, axis=-1, keepdims=True) + EPS)


def _ffn_kernel(*refs, mixer_out):
    if mixer_out:
        x_ref, a_ref, att_ref, wo_ref, g_ref, wg_ref, wu_ref, wd_ref, o_ref, h_ref = refs
    else:
        x_ref, g_ref, wg_ref, wu_ref, wd_ref, o_ref, h_ref = refs
    for r0 in range(0, x_ref.shape[0], FFN_ROWS):
        rows = slice(r0, r0 + FFN_ROWS)
        x = x_ref[rows, :]
        if mixer_out:
            x = (x + jnp.dot(a_ref[rows, :], wo_ref[0:D_CONV, :].astype(BF16),
                             preferred_element_type=F32)
                 + jnp.dot(att_ref[rows, :], wo_ref[D_CONV:, :].astype(BF16),
                           preferred_element_type=F32))
        xg = (x * g_ref[...]).astype(BF16)
        scale = _rms_scale(x)
        for c in range(D_FF // FF_CHUNK):
            sl = slice(c * FF_CHUNK, (c + 1) * FF_CHUNK)
            g = scale * jnp.dot(xg, wg_ref[:, sl].astype(BF16), preferred_element_type=F32)
            u = scale * jnp.dot(xg, wu_ref[:, sl].astype(BF16), preferred_element_type=F32)
            h_ref[rows, sl] = (g * jax.nn.sigmoid(g) * u).astype(BF16)
        y = jnp.dot(h_ref[rows, :], wd_ref[...].astype(BF16), preferred_element_type=F32)
        o_ref[rows, :] = x + FFN_RES * y


def _ffn(x2d, gain, w_gate, w_up, w_down, layer, mixer_out=None):
    n_tok = x2d.shape[0]
    tm = FFN_ROWS if mixer_out is not None else 2 * FFN_ROWS
    row = lambda i: (i, 0)
    args = [x2d]
    in_specs = [pl.BlockSpec((tm, D_MODEL), row)]
    if mixer_out is not None:
        a2d, o2d, w_out, idx = mixer_out
        args += [a2d, o2d, w_out]
        in_specs += [
            pl.BlockSpec((tm, D_CONV), row),
            pl.BlockSpec((tm, D_ATTN), row),
            _layer_spec(w_out, idx),
        ]
    args += [gain.reshape(1, D_MODEL), w_gate, w_up, w_down]
    in_specs += [
        _const_spec((1, D_MODEL)),
        _layer_spec(w_gate, layer),
        _layer_spec(w_up, layer),
        _layer_spec(w_down, layer),
    ]
    return pl.pallas_call(
        functools.partial(_ffn_kernel, mixer_out=mixer_out is not None),
        grid=(n_tok // tm,),
        in_specs=in_specs,
        out_specs=pl.BlockSpec((tm, D_MODEL), row),
        out_shape=jax.ShapeDtypeStruct((n_tok, D_MODEL), F32),
        scratch_shapes=[pltpu.VMEM((tm, D_FF), BF16)],
        compiler_params=pltpu.CompilerParams(
            dimension_semantics=("arbitrary",), vmem_limit_bytes=VMEM_LIMIT),
        name="ffn_mix" if mixer_out is not None else "ffn",
    )(*args)


def _split3_bf16(x):
    hi = x.astype(BF16)
    r1 = x - hi.astype(F32)
    mid = r1.astype(BF16)
    lo = (r1 - mid.astype(F32)).astype(BF16)
    return hi, mid, lo


def _even_in_kernel(x_ref, g_ref, w_ref, wf_ref, bf_ref, cw_ref, cb_ref, cn_ref, qg_ref, kg_ref,
                    grp_ref, tri_ref, sel_ref, a_ref, qt_ref, kx_ref, vx_ref, abuf, fcarry, sbuf,
                    cbuf):
    i = pl.program_id(1)
    tm = x_ref.shape[1]
    x = x_ref[0]
    h = (x * _rms_scale(x) * g_ref[...]).astype(BF16)

    def proj(lo, width):
        return jnp.dot(h, w_ref[:, lo:lo + width].astype(BF16), preferred_element_type=F32)

    @pl.when(i == 0)
    def _():
        abuf[0:HALO_A, :] = jnp.zeros((HALO_A, D_CONV), F32)
        fcarry[...] = jnp.zeros_like(fcarry)

    @pl.when(i > 0)
    def _():
        abuf[0:HALO_A, :] = abuf[tm:tm + HALO_A, :]

    base = HALO_A - (CONV_A_WIDTH - 1)

    def glu(cb):
        cs = slice(cb * CONV_CH, (cb + 1) * CONV_CH)
        u = proj(cb * CONV_CH, CONV_CH)
        gate = proj(D_CONV + cb * CONV_CH, CONV_CH)
        abuf[HALO_A:HALO_A + tm, cs] = u * jax.nn.sigmoid(gate)

    def conv_block(cb):
        cs = slice(cb * CONV_CH, (cb + 1) * CONV_CH)
        for shift in range(1, SUBLANES):
            sbuf[shift - 1, :, cs] = abuf[shift:shift + tm + HALO_A - SUBLANES, cs]
        for r0 in range(0, tm, CONV_ROWS):
            conv = jnp.zeros((CONV_ROWS, CONV_CH), F32) + cb_ref[:, cs]
            for t in range(CONV_A_WIDTH):
                shift = (base + t) % SUBLANES
                lo = base + t - shift + r0
                rows = (sbuf[shift - 1, lo:lo + CONV_ROWS, cs] if shift
                        else abuf[lo:lo + CONV_ROWS, cs])
                conv = conv + cw_ref[t:t + 1, cs] * rows
            cbuf[r0:r0 + CONV_ROWS, cs] = conv

    def head_norm(z, gain_ref):
        ss = jnp.dot((z * z).astype(BF16), grp_ref[...], preferred_element_type=F32)
        return z * lax.rsqrt(ss * (1.0 / HEAD_DIM) + EPS) * gain_ref[...]

    def q_part():
        qt_ref[0] = head_norm(proj(2 * D_CONV, D_ATTN), qg_ref).T.astype(BF16)

    def k_part():
        kn = head_norm(proj(2 * D_CONV + D_ATTN, D_ATTN), kg_ref).astype(BF16)
        for p in range(N_HEADS // 2):
            kx_ref[0, :, p * KX_W:p * KX_W + LANES] = kn[:, p * LANES:(p + 1) * LANES]

    def v_part():
        vt = proj(2 * D_CONV + 2 * D_ATTN, D_ATTN).T
        ones = jnp.ones((V_ROWS - HEAD_DIM, tm), BF16)
        for hd in range(N_HEADS):
            vx_ref[0, hd, 0, 0:HEAD_DIM, :] = vt[hd * HEAD_DIM:(hd + 1) * HEAD_DIM, :].astype(BF16)
            vx_ref[0, hd, 0, HEAD_DIM:, :] = ones

    def decay_part():
        fl = jnp.dot(h, wf_ref[...].astype(BF16), preferred_element_type=F32) + bf_ref[...]
        logf = jnp.minimum(fl, 0.0) - jnp.log1p(jnp.exp(-jnp.abs(fl)))
        hi, mid, lo = _split3_bf16(logf)
        tri = tri_ref[...]
        cum = (jnp.dot(tri, hi, preferred_element_type=F32)
               + jnp.dot(tri, mid, preferred_element_type=F32)
               + jnp.dot(tri, lo, preferred_element_type=F32)) + fcarry[...]
        fcarry[...] = cum[tm - 1:tm, :]
        pieces = jnp.concatenate(_split3_bf16(cum * LOG2E), axis=1)
        ext = jnp.dot(pieces, sel_ref[...], preferred_element_type=F32).astype(BF16)
        for p in range(N_HEADS // 2):
            kx_ref[0, :, p * KX_W + LANES:(p + 1) * KX_W] = ext[:, p * LANES:(p + 1) * LANES]

    glu(0)
    glu(1)
    conv_block(0)
    q_part()
    k_part()
    v_part()
    decay_part()
    conv_block(1)
    conv = cbuf[...]
    ss = jnp.dot((conv * conv).astype(BF16), jnp.ones((D_CONV, LANES), BF16),
                 preferred_element_type=F32)[:, 0:1]
    an = conv * lax.rsqrt(ss * (1.0 / D_CONV) + EPS) * cn_ref[...]
    a_ref[0] = (an * jax.nn.sigmoid(an)).astype(BF16)


def _decay_selector():
    sel = np.zeros((3 * F_PAD, (N_HEADS // 2) * LANES), np.float32)
    for piece in range(3):
        for hd in range(N_HEADS):
            sel[piece * F_PAD + hd, (hd // 2) * LANES + 3 * (hd % 2) + piece] = 1.0
    return jnp.asarray(sel, BF16)


def _even_in(x, gain, w_in, idx, b_f, conv_w, conv_b, conv_norm, q_norm, k_norm):
    bsz, seq, _ = x.shape
    tm = BK
    n_pairs = N_HEADS // 2
    scale = LOG2E / math.sqrt(HEAD_DIM)
    w_f = jnp.pad(w_in[idx, :, D_IN_EVEN - N_HEADS:], ((0, 0), (0, F_PAD - N_HEADS)))
    bf_pad = jnp.pad(b_f, (0, F_PAD - N_HEADS)).reshape(1, F_PAD)
    cw_pad = jnp.pad(conv_w, ((0, 1), (0, 0)))
    qg = (jnp.tile(q_norm, N_HEADS) * scale).reshape(1, D_ATTN)
    kg = jnp.tile(k_norm, N_HEADS).reshape(1, D_ATTN)
    head_of = jnp.arange(D_ATTN) // HEAD_DIM
    grp = (head_of[:, None] == head_of[None, :]).astype(BF16)
    tri = (jnp.arange(tm)[:, None] >= jnp.arange(tm)[None, :]).astype(BF16)
    tok = lambda b, i: (b, i, 0)
    return pl.pallas_call(
        _even_in_kernel,
        grid=(bsz, seq // tm),
        in_specs=[
            pl.BlockSpec((1, tm, D_MODEL), tok),
            _const_spec((1, D_MODEL)),
            _layer_spec(w_in, idx),
            _const_spec((D_MODEL, F_PAD)),
            _const_spec((1, F_PAD)),
            _const_spec((CONV_A_WIDTH + 1, D_CONV)),
            _const_spec((1, D_CONV)),
            _const_spec((1, D_CONV)),
            _const_spec((1, D_ATTN)),
            _const_spec((1, D_ATTN)),
            _const_spec((D_ATTN, D_ATTN)),
            _const_spec((tm, tm)),
            _const_spec((3 * F_PAD, n_pairs * LANES)),
        ],
        out_specs=[
            pl.BlockSpec((1, tm, D_CONV), tok),
            pl.BlockSpec((1, D_ATTN, tm), lambda b, i: (b, 0, i)),
            pl.BlockSpec((1, tm, n_pairs * KX_W), tok),
            pl.BlockSpec((1, N_HEADS, 1, V_ROWS, tm), lambda b, i: (b, 0, i, 0, 0)),
        ],
        out_shape=[
            jax.ShapeDtypeStruct((bsz, seq, D_CONV), BF16),
            jax.ShapeDtypeStruct((bsz, D_ATTN, seq), BF16),
            jax.ShapeDtypeStruct((bsz, seq, n_pairs * KX_W), BF16),
            jax.ShapeDtypeStruct((bsz, N_HEADS, seq // tm, V_ROWS, tm), BF16),
        ],
        scratch_shapes=[pltpu.VMEM((tm + HALO_A, D_CONV), F32), pltpu.VMEM((1, F_PAD), F32),
                        pltpu.VMEM((SUBLANES - 1, tm + HALO_A - SUBLANES, D_CONV), F32),
                        pltpu.VMEM((tm, D_CONV), F32)],
        compiler_params=pltpu.CompilerParams(
            dimension_semantics=("arbitrary", "arbitrary"), vmem_limit_bytes=VMEM_LIMIT),
        name="even_in",
    )(x, gain.reshape(1, D_MODEL), w_in, w_f, bf_pad, cw_pad, conv_b.reshape(1, D_CONV),
      conv_norm.reshape(1, D_CONV), qg, kg, grp, tri, _decay_selector())


def _attn_kernel(qt_ref, kx_ref, vx_ref, o_ref, acc_ref):
    seq = kx_ref.shape[1]
    nq = seq // BQ
    zeros = jnp.zeros((HEAD_DIM, BQ), BF16)
    r = lax.broadcasted_iota(jnp.int32, (LANES, BQ), 0)
    ext = [jnp.where((r >= 3 * hd) & (r < 3 * hd + 3), -1.0, 0.0).astype(BF16) for hd in range(2)]
    key = lax.broadcasted_iota(jnp.int32, (KH, KH), 0)
    qry = lax.broadcasted_iota(jnp.int32, (KH, KH), 1)
    causal = key <= qry
    rhs_cache = {}

    def rhs(i, hd):
        if (i, hd) not in rhs_cache:
            qt = qt_ref[0, :, i * BQ:(i + 1) * BQ]
            top = [qt[0:HEAD_DIM], zeros] if hd == 0 else [zeros, qt[HEAD_DIM:]]
            rhs_cache[(i, hd)] = jnp.concatenate(top + [ext[hd]], axis=0)
        return rhs_cache[(i, hd)]

    def logits(i, j, kh, hd):
        kx = kx_ref[0, j * BK + kh * KH:j * BK + (kh + 1) * KH, :]
        if j < i:
            return jnp.dot(kx, rhs(i, hd), preferred_element_type=F32)
        if kh == 0:
            st = jnp.dot(kx, rhs(i, hd), preferred_element_type=F32)
            return jnp.concatenate([jnp.where(causal, st[:, :KH], -jnp.inf), st[:, KH:]], axis=1)
        st = jnp.dot(kx, rhs(i, hd)[:, KH:], preferred_element_type=F32)
        return jnp.where(causal, st, -jnp.inf)

    units = [(i, j, kh, hd) for i in range(nq) for j in range(i + 1) for kh in range(2)
             for hd in range(2)]
    m = [None, None]
    sts = {k: logits(*units[k]) for k in range(min(PIPE_AHEAD, len(units)))}
    for k, (i, j, kh, hd) in enumerate(units):
        if k + PIPE_AHEAD < len(units):
            sts[k + PIPE_AHEAD] = logits(*units[k + PIPE_AHEAD])
        st = sts.pop(k)
        slot = i % 2
        first = j == 0 and kh == 0
        right_only = j == i and kh == 1
        cols = slice(KH, BQ) if right_only else slice(0, BQ)
        col_max = jnp.max(st, axis=0, keepdims=True)
        if first:
            m_new = col_max
        else:
            m_prev = m[hd][:, cols]
            m_new = jnp.maximum(m_prev, col_max)
            alpha = jnp.exp2(m_prev - m_new)
        pt = jnp.exp2(st - m_new).astype(BF16)
        vx = vx_ref[0, hd, j, :, kh * KH:(kh + 1) * KH]
        pv = jnp.dot(vx, pt, preferred_element_type=F32)
        if first:
            acc_ref[slot, hd] = pv
            m[hd] = m_new
        elif right_only:
            acc_ref[slot, hd, :, KH:] = acc_ref[slot, hd, :, KH:] * alpha + pv
            m[hd] = jnp.concatenate([m[hd][:, :KH], m_new], axis=1)
        else:
            acc_ref[slot, hd] = acc_ref[slot, hd] * alpha + pv
            m[hd] = m_new
        if right_only and hd == 1:
            parts = []
            for h2 in range(2):
                acc = acc_ref[slot, h2]
                parts.append(acc[0:HEAD_DIM] * (1.0 / acc[HEAD_DIM:HEAD_DIM + 1]))
            o_ref[0, i * BQ:(i + 1) * BQ, :] = jnp.concatenate(parts, axis=0).T.astype(BF16)


def _attention(qt, kx, vx):
    bsz, _, seq = qt.shape
    n_pairs = N_HEADS // 2
    return pl.pallas_call(
        _attn_kernel,
        grid=(bsz, n_pairs),
        in_specs=[
            pl.BlockSpec((1, 2 * HEAD_DIM, seq), lambda b, p: (b, p, 0)),
            pl.BlockSpec((1, seq, KX_W), lambda b, p: (b, 0, p)),
            pl.BlockSpec((1, 2, seq // BK, V_ROWS, BK), lambda b, p: (b, p, 0, 0, 0)),
        ],
        out_specs=pl.BlockSpec((1, seq, LANES), lambda b, p: (b, 0, p)),
        out_shape=jax.ShapeDtypeStruct((bsz, seq, D_ATTN), BF16),
        scratch_shapes=[pltpu.VMEM((2, 2, V_ROWS, BQ), F32)],
        compiler_params=pltpu.CompilerParams(
            dimension_semantics=("arbitrary", "arbitrary"), vmem_limit_bytes=VMEM_LIMIT),
        name="fox_attention",
    )(qt, kx, vx)


def _odd_kernel(x_ref, g_ref, wi_ref, cw_ref, wo_ref, y_ref, mbuf):
    i = pl.program_id(1)
    tm = x_ref.shape[1]
    x = x_ref[0]
    h = (x * _rms_scale(x) * g_ref[...]).astype(BF16)

    def proj(k):
        return jnp.dot(h, wi_ref[:, k * D_SHORT:(k + 1) * D_SHORT].astype(BF16),
                       preferred_element_type=F32)

    @pl.when(i == 0)
    def _():
        mbuf[0:HALO_C, :] = jnp.zeros((HALO_C, D_SHORT), F32)

    @pl.when(i > 0)
    def _():
        mbuf[0:HALO_C, :] = mbuf[tm:tm + HALO_C, :]

    mbuf[HALO_C:HALO_C + tm, :] = proj(1) * proj(2)
    base = HALO_C - (CONV_C_WIDTH - 1)
    conv = cw_ref[0:1, :] * mbuf[base:base + tm, :]
    for t in range(1, CONV_C_WIDTH):
        conv = conv + cw_ref[t:t + 1, :] * mbuf[base + t:base + t + tm, :]
    y = (proj(0) * conv).astype(BF16)
    y_ref[0] = x + jnp.dot(y, wo_ref[...].astype(BF16), preferred_element_type=F32)


def _odd_mixer(x, gain, w_in, conv_w, w_out, idx):
    bsz, seq, _ = x.shape
    tm = TM_MIX
    tok = lambda b, i: (b, i, 0)
    return pl.pallas_call(
        _odd_kernel,
        grid=(bsz, seq // tm),
        in_specs=[
            pl.BlockSpec((1, tm, D_MODEL), tok),
            _const_spec((1, D_MODEL)),
            _layer_spec(w_in, idx),
            _const_spec((CONV_C_WIDTH + 1, D_SHORT)),
            _layer_spec(w_out, idx),
        ],
        out_specs=pl.BlockSpec((1, tm, D_MODEL), tok),
        out_shape=jax.ShapeDtypeStruct((bsz, seq, D_MODEL), F32),
        scratch_shapes=[pltpu.VMEM((tm + HALO_C, D_SHORT), F32)],
        compiler_params=pltpu.CompilerParams(
            dimension_semantics=("arbitrary", "arbitrary"), vmem_limit_bytes=VMEM_LIMIT),
        name="odd_mixer",
    )(x, gain.reshape(1, D_MODEL), w_in, jnp.pad(conv_w, ((0, 1), (0, 0))), w_out)


def kernel(x, ffn1_norm, ffn1_w_gate, ffn1_w_up, ffn1_w_down, mix_norm, ffn2_norm, ffn2_w_gate,
           ffn2_w_up, ffn2_w_down, ev_w_in, ev_b_f, ev_conv_w, ev_conv_b, ev_conv_norm, ev_q_norm,
           ev_k_norm, ev_w_out, od_w_in, od_conv_w, od_w_out):
    bsz, seq, d = x.shape
    n_tok = bsz * seq
    depth = ffn1_norm.shape[0]
    for layer in range(depth):
        x = _ffn(x.reshape(n_tok, d), ffn1_norm[layer], ffn1_w_gate, ffn1_w_up, ffn1_w_down,
                 layer).reshape(bsz, seq, d)
        i = layer // 2
        mixer_out = None
        if layer % 2 == 0:
            a, qt, kx, vx = _even_in(x, mix_norm[layer], ev_w_in, i, ev_b_f[i], ev_conv_w[i],
                                     ev_conv_b[i], ev_conv_norm[i], ev_q_norm[i], ev_k_norm[i])
            o = _attention(qt, kx, vx)
            mixer_out = (a.reshape(n_tok, D_CONV), o.reshape(n_tok, D_ATTN), ev_w_out, i)
        else:
            x = _odd_mixer(x, mix_norm[layer], od_w_in, od_conv_w[i], od_w_out, i)
        x = _ffn(x.reshape(n_tok, d), ffn2_norm[layer], ffn2_w_gate, ffn2_w_up, ffn2_w_down,
                 layer, mixer_out).reshape(bsz, seq, d)
    return x
```

```python
import functools
import math

import jax
import jax.numpy as jnp
import numpy as np
from jax import lax
from jax.experimental import pallas as pl
from jax.experimental.pallas import tpu as pltpu

F32 = jnp.float32
BF16 = jnp.bfloat16

D_MODEL = 1024
D_FF = 2816
FFN_RES = 0.5
D_CONV = 512
CONV_A_WIDTH = 31
N_HEADS = 8
HEAD_DIM = 64
D_ATTN = N_HEADS * HEAD_DIM
CONV_C_WIDTH = 3
D_SHORT = 1024
EPS = 1e-6

LANES = 128
SUBLANES = 8
F_PAD = LANES
D_IN_EVEN = 2 * D_CONV + 3 * D_ATTN + N_HEADS

FFN_ROWS = 512
FF_CHUNK = 256
TM_MIX = 512
HALO_A = 32
CONV_ROWS = 64
CONV_CH = 256
HALO_C = 8
BQ = 512
KH = 256
PIPE_AHEAD = 2
LOG2E = 1.4426950408889634
BK = 512
KX_W = 2 * LANES
V_ROWS = 96
V7X_VMEM_BYTES = 64 * 1024 * 1024
VMEM_LIMIT = V7X_VMEM_BYTES * 7 // 8


def _const_spec(shape):
    nd = len(shape)
    return pl.BlockSpec(shape, lambda *_: (0,) * nd, pipeline_mode=pl.Buffered(1))


def _layer_spec(stacked, layer):
    _, rows, cols = stacked.shape
    return pl.BlockSpec((None, rows, cols), lambda *_: (layer, 0, 0), pipeline_mode=pl.Buffered(1))


def _rms_scale(x):
    return lax.rsqrt(jnp.mean(x * x, axis=-1, keepdims=True) + EPS)


def _ffn_kernel(*refs, mixer_out, layer):
    if mixer_out:
        (x_ref, a_ref, att_ref, wo_ref, g_ref, wg_hbm, wu_hbm, wd_hbm, o_ref,
         h_ref, wg_ref, wu_ref, wd_ref, sem) = refs
    else:
        x_ref, g_ref, wg_hbm, wu_hbm, wd_hbm, o_ref, h_ref, wg_ref, wu_ref, wd_ref, sem = refs
    n_chunks = D_FF // FF_CHUNK
    chunk = lambda c: slice(c * FF_CHUNK, (c + 1) * FF_CHUNK)

    def weight_copies(c):
        return (
            pltpu.make_async_copy(wg_hbm.at[layer, :, chunk(c)], wg_ref.at[:, chunk(c)], sem.at[c]),
            pltpu.make_async_copy(wu_hbm.at[layer, :, chunk(c)], wu_ref.at[:, chunk(c)],
                                  sem.at[n_chunks + c]),
            pltpu.make_async_copy(wd_hbm.at[layer, chunk(c), :], wd_ref.at[chunk(c), :],
                                  sem.at[2 * n_chunks + c]),
        )

    def body(first_step):
        if first_step:
            for c in range(n_chunks):
                for cp in weight_copies(c)[:2]:
                    cp.start()
            for c in range(n_chunks):
                weight_copies(c)[2].start()
        for r0 in range(0, x_ref.shape[0], FFN_ROWS):
            rows = slice(r0, r0 + FFN_ROWS)
            fetch = first_step and r0 == 0
            x = x_ref[rows, :]
            if mixer_out:
                x = (x + jnp.dot(a_ref[rows, :], wo_ref[0:D_CONV, :].astype(BF16),
                                 preferred_element_type=F32)
                     + jnp.dot(att_ref[rows, :], wo_ref[D_CONV:, :].astype(BF16),
                               preferred_element_type=F32))
            xg = (x * g_ref[...]).astype(BF16)
            scale = _rms_scale(x)
            for c in range(n_chunks):
                if fetch:
                    for cp in weight_copies(c)[:2]:
                        cp.wait()
                sl = chunk(c)
                g = scale * jnp.dot(xg, wg_ref[:, sl].astype(BF16), preferred_element_type=F32)
                u = scale * jnp.dot(xg, wu_ref[:, sl].astype(BF16), preferred_element_type=F32)
                h_ref[:, sl] = (g * jax.nn.sigmoid(g) * u).astype(BF16)
            if fetch:
                for c in range(n_chunks):
                    weight_copies(c)[2].wait()
            y = jnp.dot(h_ref[...], wd_ref[...].astype(BF16), preferred_element_type=F32)
            o_ref[rows, :] = x + FFN_RES * y

    is_first = pl.program_id(0) == 0
    pl.when(is_first)(functools.partial(body, True))
    pl.when(jnp.logical_not(is_first))(functools.partial(body, False))


def _ffn(x2d, gain, w_gate, w_up, w_down, layer, mixer_out=None):
    n_tok = x2d.shape[0]
    tm = FFN_ROWS if mixer_out is not None else 2 * FFN_ROWS
    row = lambda i: (i, 0)
    hbm = pl.BlockSpec(memory_space=pl.ANY)
    args = [x2d]
    in_specs = [pl.BlockSpec((tm, D_MODEL), row)]
    if mixer_out is not None:
        a2d, o2d, w_out, idx = mixer_out
        args += [a2d, o2d, w_out]
        in_specs += [
            pl.BlockSpec((tm, D_CONV), row),
            pl.BlockSpec((tm, D_ATTN), row),
            _layer_spec(w_out, idx),
        ]
    args += [gain.reshape(1, D_MODEL), w_gate, w_up, w_down]
    in_specs += [_const_spec((1, D_MODEL)), hbm, hbm, hbm]
    return pl.pallas_call(
        functools.partial(_ffn_kernel, mixer_out=mixer_out is not None, layer=layer),
        grid=(n_tok // tm,),
        in_specs=in_specs,
        out_specs=pl.BlockSpec((tm, D_MODEL), row),
        out_shape=jax.ShapeDtypeStruct((n_tok, D_MODEL), F32),
        scratch_shapes=[
            pltpu.VMEM((FFN_ROWS, D_FF), BF16),
            pltpu.VMEM((D_MODEL, D_FF), F32),
            pltpu.VMEM((D_MODEL, D_FF), F32),
            pltpu.VMEM((D_FF, D_MODEL), F32),
            pltpu.SemaphoreType.DMA((3 * (D_FF // FF_CHUNK),)),
        ],
        compiler_params=pltpu.CompilerParams(
            dimension_semantics=("arbitrary",), vmem_limit_bytes=VMEM_LIMIT),
        name="ffn_mix" if mixer_out is not None else "ffn",
    )(*args)


def _split3_bf16(x):
    hi = x.astype(BF16)
    r1 = x - hi.astype(F32)
    mid = r1.astype(BF16)
    lo = (r1 - mid.astype(F32)).astype(BF16)
    return hi, mid, lo


def _even_in_kernel(x_ref, g_ref, w_ref, wf_ref, bf_ref, cw_ref, cb_ref, cn_ref, qg_ref, kg_ref,
                    grp_ref, tri_ref, sel_ref, a_ref, qt_ref, kx_ref, vx_ref, abuf, fcarry, sbuf,
                    cbuf):
    i = pl.program_id(1)
    tm = x_ref.shape[1]
    x = x_ref[0]
    h = (x * _rms_scale(x) * g_ref[...]).astype(BF16)

    def proj(lo, width):
        return jnp.dot(h, w_ref[:, lo:lo + width].astype(BF16), preferred_element_type=F32)

    @pl.when(i == 0)
    def _():
        abuf[0:HALO_A, :] = jnp.zeros((HALO_A, D_CONV), F32)
        fcarry[...] = jnp.zeros_like(fcarry)

    @pl.when(i > 0)
    def _():
        abuf[0:HALO_A, :] = abuf[tm:tm + HALO_A, :]

    base = HALO_A - (CONV_A_WIDTH - 1)

    def glu(cb):
        cs = slice(cb * CONV_CH, (cb + 1) * CONV_CH)
        u = proj(cb * CONV_CH, CONV_CH)
        gate = proj(D_CONV + cb * CONV_CH, CONV_CH)
        abuf[HALO_A:HALO_A + tm, cs] = u * jax.nn.sigmoid(gate)

    def conv_block(cb):
        cs = slice(cb * CONV_CH, (cb + 1) * CONV_CH)
        for shift in range(1, SUBLANES):
            sbuf[shift - 1, :, cs] = abuf[shift:shift + tm + HALO_A - SUBLANES, cs]
        for r0 in range(0, tm, CONV_ROWS):
            conv = jnp.zeros((CONV_ROWS, CONV_CH), F32) + cb_ref[:, cs]
            for t in range(CONV_A_WIDTH):
                shift = (base + t) % SUBLANES
                lo = base + t - shift + r0
                rows = (sbuf[shift - 1, lo:lo + CONV_ROWS, cs] if shift
                        else abuf[lo:lo + CONV_ROWS, cs])
                conv = conv + cw_ref[t:t + 1, cs] * rows
            cbuf[r0:r0 + CONV_ROWS, cs] = conv

    def head_norm(z, gain_ref):
        ss = jnp.dot((z * z).astype(BF16), grp_ref[...], preferred_element_type=F32)
        return z * lax.rsqrt(ss * (1.0 / HEAD_DIM) + EPS) * gain_ref[...]

    def q_part():
        qt_ref[0] = head_norm(proj(2 * D_CONV, D_ATTN), qg_ref).T.astype(BF16)

    def k_part():
        kn = head_norm(proj(2 * D_CONV + D_ATTN, D_ATTN), kg_ref).astype(BF16)
        for p in range(N_HEADS // 2):
            kx_ref[0, :, p * KX_W:p * KX_W + LANES] = kn[:, p * LANES:(p + 1) * LANES]

    def v_part():
        vt = proj(2 * D_CONV + 2 * D_ATTN, D_ATTN).T
        ones = jnp.ones((V_ROWS - HEAD_DIM, tm), BF16)
        for hd in range(N_HEADS):
            vx_ref[0, hd, 0, 0:HEAD_DIM, :] = vt[hd * HEAD_DIM:(hd + 1) * HEAD_DIM, :].astype(BF16)
            vx_ref[0, hd, 0, HEAD_DIM:, :] = ones

    def decay_part():
        fl = jnp.dot(h, wf_ref[...].astype(BF16), preferred_element_type=F32) + bf_ref[...]
        logf = jnp.minimum(fl, 0.0) - jnp.log1p(jnp.exp(-jnp.abs(fl)))
        hi, mid, lo = _split3_bf16(logf)
        tri = tri_ref[...]
        cum = (jnp.dot(tri, hi, preferred_element_type=F32)
               + jnp.dot(tri, mid, preferred_element_type=F32)
               + jnp.dot(tri, lo, preferred_element_type=F32)) + fcarry[...]
        fcarry[...] = cum[tm - 1:tm, :]
        pieces = jnp.concatenate(_split3_bf16(cum * LOG2E), axis=1)
        ext = jnp.dot(pieces, sel_ref[...], preferred_element_type=F32).astype(BF16)
        for p in range(N_HEADS // 2):
            kx_ref[0, :, p * KX_W + LANES:(p + 1) * KX_W] = ext[:, p * LANES:(p + 1) * LANES]

    glu(0)
    glu(1)
    conv_block(0)
    q_part()
    k_part()
    v_part()
    decay_part()
    conv_block(1)
    conv = cbuf[...]
    ss = jnp.dot((conv * conv).astype(BF16), jnp.ones((D_CONV, LANES), BF16),
                 preferred_element_type=F32)[:, 0:1]
    an = conv * lax.rsqrt(ss * (1.0 / D_CONV) + EPS) * cn_ref[...]
    a_ref[0] = (an * jax.nn.sigmoid(an)).astype(BF16)


def _decay_selector():
    sel = np.zeros((3 * F_PAD, (N_HEADS // 2) * LANES), np.float32)
    for piece in range(3):
        for hd in range(N_HEADS):
            sel[piece * F_PAD + hd, (hd // 2) * LANES + 3 * (hd % 2) + piece] = 1.0
    return jnp.asarray(sel, BF16)


def _even_in(x, gain, w_in, idx, b_f, conv_w, conv_b, conv_norm, q_norm, k_norm):
    bsz, seq, _ = x.shape
    tm = BK
    n_pairs = N_HEADS // 2
    scale = LOG2E / math.sqrt(HEAD_DIM)
    w_f = jnp.pad(w_in[idx, :, D_IN_EVEN - N_HEADS:], ((0, 0), (0, F_PAD - N_HEADS)))
    bf_pad = jnp.pad(b_f, (0, F_PAD - N_HEADS)).reshape(1, F_PAD)
    cw_pad = jnp.pad(conv_w, ((0, 1), (0, 0)))
    qg = (jnp.tile(q_norm, N_HEADS) * scale).reshape(1, D_ATTN)
    kg = jnp.tile(k_norm, N_HEADS).reshape(1, D_ATTN)
    head_of = jnp.arange(D_ATTN) // HEAD_DIM
    grp = (head_of[:, None] == head_of[None, :]).astype(BF16)
    tri = (jnp.arange(tm)[:, None] >= jnp.arange(tm)[None, :]).astype(BF16)
    tok = lambda b, i: (b, i, 0)
    return pl.pallas_call(
        _even_in_kernel,
        grid=(bsz, seq // tm),
        in_specs=[
            pl.BlockSpec((1, tm, D_MODEL), tok),
            _const_spec((1, D_MODEL)),
            _layer_spec(w_in, idx),
            _const_spec((D_MODEL, F_PAD)),
            _const_spec((1, F_PAD)),
            _const_spec((CONV_A_WIDTH + 1, D_CONV)),
            _const_spec((1, D_CONV)),
            _const_spec((1, D_CONV)),
            _const_spec((1, D_ATTN)),
            _const_spec((1, D_ATTN)),
            _const_spec((D_ATTN, D_ATTN)),
            _const_spec((tm, tm)),
            _const_spec((3 * F_PAD, n_pairs * LANES)),
        ],
        out_specs=[
            pl.BlockSpec((1, tm, D_CONV), tok),
            pl.BlockSpec((1, D_ATTN, tm), lambda b, i: (b, 0, i)),
            pl.BlockSpec((1, tm, n_pairs * KX_W), tok),
            pl.BlockSpec((1, N_HEADS, 1, V_ROWS, tm), lambda b, i: (b, 0, i, 0, 0)),
        ],
        out_shape=[
            jax.ShapeDtypeStruct((bsz, seq, D_CONV), BF16),
            jax.ShapeDtypeStruct((bsz, D_ATTN, seq), BF16),
            jax.ShapeDtypeStruct((bsz, seq, n_pairs * KX_W), BF16),
            jax.ShapeDtypeStruct((bsz, N_HEADS, seq // tm, V_ROWS, tm), BF16),
        ],
        scratch_shapes=[pltpu.VMEM((tm + HALO_A, D_CONV), F32), pltpu.VMEM((1, F_PAD), F32),
                        pltpu.VMEM((SUBLANES - 1, tm + HALO_A - SUBLANES, D_CONV), F32),
                        pltpu.VMEM((tm, D_CONV), F32)],
        compiler_params=pltpu.CompilerParams(
            dimension_semantics=("arbitrary", "arbitrary"), vmem_limit_bytes=VMEM_LIMIT),
        name="even_in",
    )(x, gain.reshape(1, D_MODEL), w_in, w_f, bf_pad, cw_pad, conv_b.reshape(1, D_CONV),
      conv_norm.reshape(1, D_CONV), qg, kg, grp, tri, _decay_selector())


def _attn_kernel(qt_ref, kx_ref, vx_ref, o_ref, acc_ref):
    seq = kx_ref.shape[1]
    nq = seq // BQ
    zeros = jnp.zeros((HEAD_DIM, BQ), BF16)
    r = lax.broadcasted_iota(jnp.int32, (LANES, BQ), 0)
    ext = [jnp.where((r >= 3 * hd) & (r < 3 * hd + 3), -1.0, 0.0).astype(BF16) for hd in range(2)]
    key = lax.broadcasted_iota(jnp.int32, (KH, KH), 0)
    qry = lax.broadcasted_iota(jnp.int32, (KH, KH), 1)
    causal = key <= qry
    rhs_cache = {}

    def rhs(i, hd):
        if (i, hd) not in rhs_cache:
            qt = qt_ref[0, :, i * BQ:(i + 1) * BQ]
            top = [qt[0:HEAD_DIM], zeros] if hd == 0 else [zeros, qt[HEAD_DIM:]]
            rhs_cache[(i, hd)] = jnp.concatenate(top + [ext[hd]], axis=0)
        return rhs_cache[(i, hd)]

    def logits(i, j, kh, hd):
        kx = kx_ref[0, j * BK + kh * KH:j * BK + (kh + 1) * KH, :]
        if j < i:
            return jnp.dot(kx, rhs(i, hd), preferred_element_type=F32)
        if kh == 0:
            st = jnp.dot(kx, rhs(i, hd), preferred_element_type=F32)
            return jnp.concatenate([jnp.where(causal, st[:, :KH], -jnp.inf), st[:, KH:]], axis=1)
        st = jnp.dot(kx, rhs(i, hd)[:, KH:], preferred_element_type=F32)
        return jnp.where(causal, st, -jnp.inf)

    units = [(i, j, kh, hd) for i in range(nq) for j in range(i + 1) for kh in range(2)
             for hd in range(2)]
    m = [None, None]
    sts = {k: logits(*units[k]) for k in range(min(PIPE_AHEAD, len(units)))}
    for k, (i, j, kh, hd) in enumerate(units):
        if k + PIPE_AHEAD < len(units):
            sts[k + PIPE_AHEAD] = logits(*units[k + PIPE_AHEAD])
        st = sts.pop(k)
        slot = i % 2
        first = j == 0 and kh == 0
        right_only = j == i and kh == 1
        cols = slice(KH, BQ) if right_only else slice(0, BQ)
        col_max = jnp.max(st, axis=0, keepdims=True)
        if first:
            m_new = col_max
        else:
            m_prev = m[hd][:, cols]
            m_new = jnp.maximum(m_prev, col_max)
            alpha = jnp.exp2(m_prev - m_new)
        pt = jnp.exp2(st - m_new).astype(BF16)
        vx = vx_ref[0, hd, j, :, kh * KH:(kh + 1) * KH]
        pv = jnp.dot(vx, pt, preferred_element_type=F32)
        if first:
            acc_ref[slot, hd] = pv
            m[hd] = m_new
        elif right_only:
            acc_ref[slot, hd, :, KH:] = acc_ref[slot, hd, :, KH:] * alpha + pv
            m[hd] = jnp.concatenate([m[hd][:, :KH], m_new], axis=1)
        else:
            acc_ref[slot, hd] = acc_ref[slot, hd] * alpha + pv
            m[hd] = m_new
        if right_only and hd == 1:
            parts = []
            for h2 in range(2):
                acc = acc_ref[slot, h2]
                parts.append(acc[0:HEAD_DIM] * (1.0 / acc[HEAD_DIM:HEAD_DIM + 1]))
            o_ref[0, i * BQ:(i + 1) * BQ, :] = jnp.concatenate(parts, axis=0).T.astype(BF16)


def _attention(qt, kx, vx):
    bsz, _, seq = qt.shape
    n_pairs = N_HEADS // 2
    return pl.pallas_call(
        _attn_kernel,
        grid=(bsz, n_pairs),
        in_specs=[
            pl.BlockSpec((1, 2 * HEAD_DIM, seq), lambda b, p: (b, p, 0)),
            pl.BlockSpec((1, seq, KX_W), lambda b, p: (b, 0, p)),
            pl.BlockSpec((1, 2, seq // BK, V_ROWS, BK), lambda b, p: (b, p, 0, 0, 0)),
        ],
        out_specs=pl.BlockSpec((1, seq, LANES), lambda b, p: (b, 0, p)),
        out_shape=jax.ShapeDtypeStruct((bsz, seq, D_ATTN), BF16),
        scratch_shapes=[pltpu.VMEM((2, 2, V_ROWS, BQ), F32)],
        compiler_params=pltpu.CompilerParams(
            dimension_semantics=("arbitrary", "arbitrary"), vmem_limit_bytes=VMEM_LIMIT),
        name="fox_attention",
    )(qt, kx, vx)


def _odd_kernel(x_ref, g_ref, wi_ref, cw_ref, wo_ref, y_ref, mbuf):
    i = pl.program_id(1)
    tm = x_ref.shape[1]
    x = x_ref[0]
    h = (x * _rms_scale(x) * g_ref[...]).astype(BF16)

    def proj(k):
        return jnp.dot(h, wi_ref[:, k * D_SHORT:(k + 1) * D_SHORT].astype(BF16),
                       preferred_element_type=F32)

    @pl.when(i == 0)
    def _():
        mbuf[0:HALO_C, :] = jnp.zeros((HALO_C, D_SHORT), F32)

    @pl.when(i > 0)
    def _():
        mbuf[0:HALO_C, :] = mbuf[tm:tm + HALO_C, :]

    mbuf[HALO_C:HALO_C + tm, :] = proj(1) * proj(2)
    base = HALO_C - (CONV_C_WIDTH - 1)
    conv = cw_ref[0:1, :] * mbuf[base:base + tm, :]
    for t in range(1, CONV_C_WIDTH):
        conv = conv + cw_ref[t:t + 1, :] * mbuf[base + t:base + t + tm, :]
    y = (proj(0) * conv).astype(BF16)
    y_ref[0] = x + jnp.dot(y, wo_ref[...].astype(BF16), preferred_element_type=F32)


def _odd_mixer(x, gain, w_in, conv_w, w_out, idx):
    bsz, seq, _ = x.shape
    tm = TM_MIX
    tok = lambda b, i: (b, i, 0)
    return pl.pallas_call(
        _odd_kernel,
        grid=(bsz, seq // tm),
        in_specs=[
            pl.BlockSpec((1, tm, D_MODEL), tok),
            _const_spec((1, D_MODEL)),
            _layer_spec(w_in, idx),
            _const_spec((CONV_C_WIDTH + 1, D_SHORT)),
            _layer_spec(w_out, idx),
        ],
        out_specs=pl.BlockSpec((1, tm, D_MODEL), tok),
        out_shape=jax.ShapeDtypeStruct((bsz, seq, D_MODEL), F32),
        scratch_shapes=[pltpu.VMEM((tm + HALO_C, D_SHORT), F32)],
        compiler_params=pltpu.CompilerParams(
            dimension_semantics=("arbitrary", "arbitrary"), vmem_limit_bytes=VMEM_LIMIT),
        name="odd_mixer",
    )(x, gain.reshape(1, D_MODEL), w_in, jnp.pad(conv_w, ((0, 1), (0, 0))), w_out)


def kernel(x, ffn1_norm, ffn1_w_gate, ffn1_w_up, ffn1_w_down, mix_norm, ffn2_norm, ffn2_w_gate,
           ffn2_w_up, ffn2_w_down, ev_w_in, ev_b_f, ev_conv_w, ev_conv_b, ev_conv_norm, ev_q_norm,
           ev_k_norm, ev_w_out, od_w_in, od_conv_w, od_w_out):
    bsz, seq, d = x.shape
    n_tok = bsz * seq
    depth = ffn1_norm.shape[0]
    for layer in range(depth):
        x = _ffn(x.reshape(n_tok, d), ffn1_norm[layer], ffn1_w_gate, ffn1_w_up, ffn1_w_down,
                 layer).reshape(bsz, seq, d)
        i = layer // 2
        mixer_out = None
        if layer % 2 == 0:
            a, qt, kx, vx = _even_in(x, mix_norm[layer], ev_w_in, i, ev_b_f[i], ev_conv_w[i],
                                     ev_conv_b[i], ev_conv_norm[i], ev_q_norm[i], ev_k_norm[i])
            o = _attention(qt, kx, vx)
            mixer_out = (a.reshape(n_tok, D_CONV), o.reshape(n_tok, D_ATTN), ev_w_out, i)
        else:
            x = _odd_mixer(x, mix_norm[layer], od_w_in, od_conv_w[i], od_w_out, i)
        x = _ffn(x.reshape(n_tok, d), ffn2_norm[layer], ffn2_w_gate, ffn2_w_up, ffn2_w_down,
                 layer, mixer_out).reshape(bsz, seq, d)
    return x
```

```python
import functools
import math

import jax
import jax.numpy as jnp
import numpy as np
from jax import lax
from jax.experimental import pallas as pl
from jax.experimental.pallas import tpu as pltpu

F32 = jnp.float32
BF16 = jnp.bfloat16

D_MODEL = 1024
D_FF = 2816
FFN_RES = 0.5
D_CONV = 512
CONV_A_WIDTH = 31
N_HEADS = 8
HEAD_DIM = 64
D_ATTN = N_HEADS * HEAD_DIM
CONV_C_WIDTH = 3
D_SHORT = 1024
EPS = 1e-6

LANES = 128
SUBLANES = 8
F_PAD = LANES
D_IN_EVEN = 2 * D_CONV + 3 * D_ATTN + N_HEADS

FFN_ROWS = 512
FF_CHUNK = 256
TM_MIX = 512
HALO_A = 32
CONV_ROWS = 64
CONV_CH = 256
HALO_C = 8
BQ = 512
KH = 256
PIPE_AHEAD = 2
LOG2E = 1.4426950408889634
BK = 512
KX_W = 2 * LANES
V_ROWS = 96
V7X_VMEM_BYTES = 64 * 1024 * 1024
VMEM_LIMIT = V7X_VMEM_BYTES * 7 // 8


def _const_spec(shape):
    nd = len(shape)
    return pl.BlockSpec(shape, lambda *_: (0,) * nd, pipeline_mode=pl.Buffered(1))


def _layer_spec(stacked, layer):
    _, rows, cols = stacked.shape
    return pl.BlockSpec((None, rows, cols), lambda *_: (layer, 0, 0), pipeline_mode=pl.Buffered(1))


def _rms_scale(x):
    return lax.rsqrt(jnp.mean(x * x, axis=-1, keepdims=True) + EPS)


def _ffn_kernel(*refs, mixer_out):
    if mixer_out:
        x_ref, a_ref, att_ref, wo_ref, g_ref, wg_ref, wu_ref, wd_ref, o_ref, h_ref = refs
    else:
        x_ref, g_ref, wg_ref, wu_ref, wd_ref, o_ref, h_ref = refs
    for r0 in range(0, x_ref.shape[0], FFN_ROWS):
        rows = slice(r0, r0 + FFN_ROWS)
        x = x_ref[rows, :]
        if mixer_out:
            x = (x + jnp.dot(a_ref[rows, :], wo_ref[0:D_CONV, :].astype(BF16),
                             preferred_element_type=F32)
                 + jnp.dot(att_ref[rows, :], wo_ref[D_CONV:, :].astype(BF16),
                           preferred_element_type=F32))
        xg = (x * g_ref[...]).astype(BF16)
        scale = _rms_scale(x)
        for c in range(D_FF // FF_CHUNK):
            sl = slice(c * FF_CHUNK, (c + 1) * FF_CHUNK)
            g = scale * jnp.dot(xg, wg_ref[:, sl].astype(BF16), preferred_element_type=F32)
            u = scale * jnp.dot(xg, wu_ref[:, sl].astype(BF16), preferred_element_type=F32)
            h_ref[rows, sl] = (g * jax.nn.sigmoid(g) * u).astype(BF16)
        y = jnp.dot(h_ref[rows, :], wd_ref[...].astype(BF16), preferred_element_type=F32)
        o_ref[rows, :] = x + FFN_RES * y


def _ffn(x2d, gain, w_gate, w_up, w_down, layer, mixer_out=None):
    n_tok = x2d.shape[0]
    tm = FFN_ROWS if mixer_out is not None else 2 * FFN_ROWS
    assert n_tok % tm == 0
    assert w_gate.shape[1:] == w_up.shape[1:] == (D_MODEL, D_FF) and w_down.shape[1:] == (D_FF, D_MODEL)
    row = lambda i: (i, 0)
    args = [x2d]
    in_specs = [pl.BlockSpec((tm, D_MODEL), row)]
    if mixer_out is not None:
        a2d, o2d, w_out, idx = mixer_out
        args += [a2d, o2d, w_out]
        in_specs += [
            pl.BlockSpec((tm, D_CONV), row),
            pl.BlockSpec((tm, D_ATTN), row),
            _layer_spec(w_out, idx),
        ]
    args += [gain.reshape(1, D_MODEL), w_gate, w_up, w_down]
    in_specs += [
        _const_spec((1, D_MODEL)),
        _layer_spec(w_gate, layer),
        _layer_spec(w_up, layer),
        _layer_spec(w_down, layer),
    ]
    return pl.pallas_call(
        functools.partial(_ffn_kernel, mixer_out=mixer_out is not None),
        grid=(n_tok // tm,),
        in_specs=in_specs,
        out_specs=pl.BlockSpec((tm, D_MODEL), row),
        out_shape=jax.ShapeDtypeStruct((n_tok, D_MODEL), F32),
        scratch_shapes=[pltpu.VMEM((tm, D_FF), BF16)],
        compiler_params=pltpu.CompilerParams(
            dimension_semantics=("arbitrary",), vmem_limit_bytes=VMEM_LIMIT),
        name="ffn_mix" if mixer_out is not None else "ffn",
    )(*args)


def _split3_bf16(x):
    hi = x.astype(BF16)
    r1 = x - hi.astype(F32)
    mid = r1.astype(BF16)
    lo = (r1 - mid.astype(F32)).astype(BF16)
    return hi, mid, lo


def _even_in_kernel(x_ref, g_ref, w_ref, wf_ref, bf_ref, cw_ref, cb_ref, cn_ref, qg_ref, kg_ref,
                    grp_ref, tri_ref, sel_ref, a_ref, qt_ref, kx_ref, vx_ref, abuf, fcarry, sbuf,
                    cbuf):
    i = pl.program_id(1)
    tm = x_ref.shape[1]
    x = x_ref[0]
    h = (x * _rms_scale(x) * g_ref[...]).astype(BF16)

    def proj(lo, width):
        return jnp.dot(h, w_ref[:, lo:lo + width].astype(BF16), preferred_element_type=F32)

    @pl.when(i == 0)
    def _():
        abuf[0:HALO_A, :] = jnp.zeros((HALO_A, D_CONV), F32)
        fcarry[...] = jnp.zeros_like(fcarry)

    @pl.when(i > 0)
    def _():
        abuf[0:HALO_A, :] = abuf[tm:tm + HALO_A, :]

    base = HALO_A - (CONV_A_WIDTH - 1)

    def glu(cb):
        cs = slice(cb * CONV_CH, (cb + 1) * CONV_CH)
        u = proj(cb * CONV_CH, CONV_CH)
        gate = proj(D_CONV + cb * CONV_CH, CONV_CH)
        abuf[HALO_A:HALO_A + tm, cs] = u * jax.nn.sigmoid(gate)

    def conv_block(cb):
        cs = slice(cb * CONV_CH, (cb + 1) * CONV_CH)
        for shift in range(1, SUBLANES):
            sbuf[shift - 1, :, cs] = abuf[shift:shift + tm + HALO_A - SUBLANES, cs]
        for r0 in range(0, tm, CONV_ROWS):
            conv = jnp.zeros((CONV_ROWS, CONV_CH), F32) + cb_ref[:, cs]
            for t in range(CONV_A_WIDTH):
                shift = (base + t) % SUBLANES
                lo = base + t - shift + r0
                rows = (sbuf[shift - 1, lo:lo + CONV_ROWS, cs] if shift
                        else abuf[lo:lo + CONV_ROWS, cs])
                conv = conv + cw_ref[t:t + 1, cs] * rows
            cbuf[r0:r0 + CONV_ROWS, cs] = conv

    def head_norm(z, gain_ref):
        ss = jnp.dot((z * z).astype(BF16), grp_ref[...], preferred_element_type=F32)
        return z * lax.rsqrt(ss * (1.0 / HEAD_DIM) + EPS) * gain_ref[...]

    def q_part():
        qt_ref[0] = head_norm(proj(2 * D_CONV, D_ATTN), qg_ref).T.astype(BF16)

    def k_part():
        kn = head_norm(proj(2 * D_CONV + D_ATTN, D_ATTN), kg_ref).astype(BF16)
        for p in range(N_HEADS // 2):
            kx_ref[0, :, p * KX_W:p * KX_W + LANES] = kn[:, p * LANES:(p + 1) * LANES]

    def v_part():
        vt = proj(2 * D_CONV + 2 * D_ATTN, D_ATTN).T
        ones = jnp.ones((V_ROWS - HEAD_DIM, tm), BF16)
        for hd in range(N_HEADS):
            vx_ref[0, hd, 0, 0:HEAD_DIM, :] = vt[hd * HEAD_DIM:(hd + 1) * HEAD_DIM, :].astype(BF16)
            vx_ref[0, hd, 0, HEAD_DIM:, :] = ones

    def decay_part():
        fl = jnp.dot(h, wf_ref[...].astype(BF16), preferred_element_type=F32) + bf_ref[...]
        logf = jnp.minimum(fl, 0.0) - jnp.log1p(jnp.exp(-jnp.abs(fl)))
        hi, mid, lo = _split3_bf16(logf)
        tri = tri_ref[...]
        cum = (jnp.dot(tri, hi, preferred_element_type=F32)
               + jnp.dot(tri, mid, preferred_element_type=F32)
               + jnp.dot(tri, lo, preferred_element_type=F32)) + fcarry[...]
        fcarry[...] = cum[tm - 1:tm, :]
        pieces = jnp.concatenate(_split3_bf16(cum * LOG2E), axis=1)
        ext = jnp.dot(pieces, sel_ref[...], preferred_element_type=F32).astype(BF16)
        for p in range(N_HEADS // 2):
            kx_ref[0, :, p * KX_W + LANES:(p + 1) * KX_W] = ext[:, p * LANES:(p + 1) * LANES]

    glu(0)
    glu(1)
    conv_block(0)
    q_part()
    k_part()
    v_part()
    decay_part()
    conv_block(1)
    conv = cbuf[...]
    ss = jnp.dot((conv * conv).astype(BF16), jnp.ones((D_CONV, LANES), BF16),
                 preferred_element_type=F32)[:, 0:1]
    an = conv * lax.rsqrt(ss * (1.0 / D_CONV) + EPS) * cn_ref[...]
    a_ref[0] = (an * jax.nn.sigmoid(an)).astype(BF16)


def _decay_selector():
    sel = np.zeros((3 * F_PAD, (N_HEADS // 2) * LANES), np.float32)
    for piece in range(3):
        for hd in range(N_HEADS):
            sel[piece * F_PAD + hd, (hd // 2) * LANES + 3 * (hd % 2) + piece] = 1.0
    return jnp.asarray(sel, BF16)


def _even_in(x, gain, w_in, idx, b_f, conv_w, conv_b, conv_norm, q_norm, k_norm):
    bsz, seq, _ = x.shape
    tm = BK
    assert seq % tm == 0 and BK == BQ and w_in.shape[1:] == (D_MODEL, D_IN_EVEN)
    n_pairs = N_HEADS // 2
    scale = LOG2E / math.sqrt(HEAD_DIM)
    w_f = jnp.pad(w_in[idx, :, D_IN_EVEN - N_HEADS:], ((0, 0), (0, F_PAD - N_HEADS)))
    bf_pad = jnp.pad(b_f, (0, F_PAD - N_HEADS)).reshape(1, F_PAD)
    cw_pad = jnp.pad(conv_w, ((0, 1), (0, 0)))
    qg = (jnp.tile(q_norm, N_HEADS) * scale).reshape(1, D_ATTN)
    kg = jnp.tile(k_norm, N_HEADS).reshape(1, D_ATTN)
    head_of = jnp.arange(D_ATTN) // HEAD_DIM
    grp = (head_of[:, None] == head_of[None, :]).astype(BF16)
    tri = (jnp.arange(tm)[:, None] >= jnp.arange(tm)[None, :]).astype(BF16)
    tok = lambda b, i: (b, i, 0)
    return pl.pallas_call(
        _even_in_kernel,
        grid=(bsz, seq // tm),
        in_specs=[
            pl.BlockSpec((1, tm, D_MODEL), tok),
            _const_spec((1, D_MODEL)),
            _layer_spec(w_in, idx),
            _const_spec((D_MODEL, F_PAD)),
            _const_spec((1, F_PAD)),
            _const_spec((CONV_A_WIDTH + 1, D_CONV)),
            _const_spec((1, D_CONV)),
            _const_spec((1, D_CONV)),
            _const_spec((1, D_ATTN)),
            _const_spec((1, D_ATTN)),
            _const_spec((D_ATTN, D_ATTN)),
            _const_spec((tm, tm)),
            _const_spec((3 * F_PAD, n_pairs * LANES)),
        ],
        out_specs=[
            pl.BlockSpec((1, tm, D_CONV), tok),
            pl.BlockSpec((1, D_ATTN, tm), lambda b, i: (b, 0, i)),
            pl.BlockSpec((1, tm, n_pairs * KX_W), tok),
            pl.BlockSpec((1, N_HEADS, 1, V_ROWS, tm), lambda b, i: (b, 0, i, 0, 0)),
        ],
        out_shape=[
            jax.ShapeDtypeStruct((bsz, seq, D_CONV), BF16),
            jax.ShapeDtypeStruct((bsz, D_ATTN, seq), BF16),
            jax.ShapeDtypeStruct((bsz, seq, n_pairs * KX_W), BF16),
            jax.ShapeDtypeStruct((bsz, N_HEADS, seq // tm, V_ROWS, tm), BF16),
        ],
        scratch_shapes=[pltpu.VMEM((tm + HALO_A, D_CONV), F32), pltpu.VMEM((1, F_PAD), F32),
                        pltpu.VMEM((SUBLANES - 1, tm + HALO_A - SUBLANES, D_CONV), F32),
                        pltpu.VMEM((tm, D_CONV), F32)],
        compiler_params=pltpu.CompilerParams(
            dimension_semantics=("arbitrary", "arbitrary"), vmem_limit_bytes=VMEM_LIMIT),
        name="even_in",
    )(x, gain.reshape(1, D_MODEL), w_in, w_f, bf_pad, cw_pad, conv_b.reshape(1, D_CONV),
      conv_norm.reshape(1, D_CONV), qg, kg, grp, tri, _decay_selector())


def _attn_kernel(qt_ref, kx_ref, vx_ref, o_ref, acc_ref):
    seq = kx_ref.shape[1]
    nq = seq // BQ
    zeros = jnp.zeros((HEAD_DIM, BQ), BF16)
    r = lax.broadcasted_iota(jnp.int32, (LANES, BQ), 0)
    ext = [jnp.where((r >= 3 * hd) & (r < 3 * hd + 3), -1.0, 0.0).astype(BF16) for hd in range(2)]
    key = lax.broadcasted_iota(jnp.int32, (KH, KH), 0)
    qry = lax.broadcasted_iota(jnp.int32, (KH, KH), 1)
    causal = key <= qry
    rhs_cache = {}

    def rhs(i, hd):
        if (i, hd) not in rhs_cache:
            qt = qt_ref[0, :, i * BQ:(i + 1) * BQ]
            top = [qt[0:HEAD_DIM], zeros] if hd == 0 else [zeros, qt[HEAD_DIM:]]
            rhs_cache[(i, hd)] = jnp.concatenate(top + [ext[hd]], axis=0)
        return rhs_cache[(i, hd)]

    def logits(i, j, kh, hd):
        kx = kx_ref[0, j * BK + kh * KH:j * BK + (kh + 1) * KH, :]
        if j < i:
            return jnp.dot(kx, rhs(i, hd), preferred_element_type=F32)
        if kh == 0:
            st = jnp.dot(kx, rhs(i, hd), preferred_element_type=F32)
            return jnp.concatenate([jnp.where(causal, st[:, :KH], -jnp.inf), st[:, KH:]], axis=1)
        st = jnp.dot(kx, rhs(i, hd)[:, KH:], preferred_element_type=F32)
        return jnp.where(causal, st, -jnp.inf)

    units = [(i, j, kh, hd) for i in range(nq) for j in range(i + 1) for kh in range(2)
             for hd in range(2)]
    m = [None, None]
    sts = {k: logits(*units[k]) for k in range(min(PIPE_AHEAD, len(units)))}
    for k, (i, j, kh, hd) in enumerate(units):
        if k + PIPE_AHEAD < len(units):
            sts[k + PIPE_AHEAD] = logits(*units[k + PIPE_AHEAD])
        st = sts.pop(k)
        slot = i % 2
        first = j == 0 and kh == 0
        right_only = j == i and kh == 1
        cols = slice(KH, BQ) if right_only else slice(0, BQ)
        col_max = jnp.max(st, axis=0, keepdims=True)
        if first:
            m_new = col_max
        else:
            m_prev = m[hd][:, cols]
            m_new = jnp.maximum(m_prev, col_max)
            alpha = jnp.exp2(m_prev - m_new)
        pt = jnp.exp2(st - m_new).astype(BF16)
        vx = vx_ref[0, hd, j, :, kh * KH:(kh + 1) * KH]
        pv = jnp.dot(vx, pt, preferred_element_type=F32)
        if first:
            acc_ref[slot, hd] = pv
            m[hd] = m_new
        elif right_only:
            acc_ref[slot, hd, :, KH:] = acc_ref[slot, hd, :, KH:] * alpha + pv
            m[hd] = jnp.concatenate([m[hd][:, :KH], m_new], axis=1)
        else:
            acc_ref[slot, hd] = acc_ref[slot, hd] * alpha + pv
            m[hd] = m_new
        if right_only and hd == 1:
            parts = []
            for h2 in range(2):
                acc = acc_ref[slot, h2]
                parts.append(acc[0:HEAD_DIM] * (1.0 / acc[HEAD_DIM:HEAD_DIM + 1]))
            o_ref[0, i * BQ:(i + 1) * BQ, :] = jnp.concatenate(parts, axis=0).T.astype(BF16)


def _attention(qt, kx, vx):
    bsz, _, seq = qt.shape
    n_pairs = N_HEADS // 2
    return pl.pallas_call(
        _attn_kernel,
        grid=(bsz, n_pairs),
        in_specs=[
            pl.BlockSpec((1, 2 * HEAD_DIM, seq), lambda b, p: (b, p, 0)),
            pl.BlockSpec((1, seq, KX_W), lambda b, p: (b, 0, p)),
            pl.BlockSpec((1, 2, seq // BK, V_ROWS, BK), lambda b, p: (b, p, 0, 0, 0)),
        ],
        out_specs=pl.BlockSpec((1, seq, LANES), lambda b, p: (b, 0, p)),
        out_shape=jax.ShapeDtypeStruct((bsz, seq, D_ATTN), BF16),
        scratch_shapes=[pltpu.VMEM((2, 2, V_ROWS, BQ), F32)],
        compiler_params=pltpu.CompilerParams(
            dimension_semantics=("arbitrary", "arbitrary"), vmem_limit_bytes=VMEM_LIMIT),
        name="fox_attention",
    )(qt, kx, vx)


def _odd_kernel(x_ref, g_ref, wi_ref, cw_ref, wo_ref, y_ref, mbuf):
    i = pl.program_id(1)
    tm = x_ref.shape[1]
    x = x_ref[0]
    h = (x * _rms_scale(x) * g_ref[...]).astype(BF16)

    def proj(k):
        return jnp.dot(h, wi_ref[:, k * D_SHORT:(k + 1) * D_SHORT].astype(BF16),
                       preferred_element_type=F32)

    @pl.when(i == 0)
    def _():
        mbuf[0:HALO_C, :] = jnp.zeros((HALO_C, D_SHORT), F32)

    @pl.when(i > 0)
    def _():
        mbuf[0:HALO_C, :] = mbuf[tm:tm + HALO_C, :]

    mbuf[HALO_C:HALO_C + tm, :] = proj(1) * proj(2)
    base = HALO_C - (CONV_C_WIDTH - 1)
    conv = cw_ref[0:1, :] * mbuf[base:base + tm, :]
    for t in range(1, CONV_C_WIDTH):
        conv = conv + cw_ref[t:t + 1, :] * mbuf[base + t:base + t + tm, :]
    y = (proj(0) * conv).astype(BF16)
    y_ref[0] = x + jnp.dot(y, wo_ref[...].astype(BF16), preferred_element_type=F32)


def _odd_mixer(x, gain, w_in, conv_w, w_out, idx):
    bsz, seq, _ = x.shape
    tm = TM_MIX
    assert seq % tm == 0 and w_in.shape[1:] == (D_MODEL, 3 * D_SHORT)
    tok = lambda b, i: (b, i, 0)
    return pl.pallas_call(
        _odd_kernel,
        grid=(bsz, seq // tm),
        in_specs=[
            pl.BlockSpec((1, tm, D_MODEL), tok),
            _const_spec((1, D_MODEL)),
            _layer_spec(w_in, idx),
            _const_spec((CONV_C_WIDTH + 1, D_SHORT)),
            _layer_spec(w_out, idx),
        ],
        out_specs=pl.BlockSpec((1, tm, D_MODEL), tok),
        out_shape=jax.ShapeDtypeStruct((bsz, seq, D_MODEL), F32),
        scratch_shapes=[pltpu.VMEM((tm + HALO_C, D_SHORT), F32)],
        compiler_params=pltpu.CompilerParams(
            dimension_semantics=("arbitrary", "arbitrary"), vmem_limit_bytes=VMEM_LIMIT),
        name="odd_mixer",
    )(x, gain.reshape(1, D_MODEL), w_in, jnp.pad(conv_w, ((0, 1), (0, 0))), w_out)


def kernel(x, ffn1_norm, ffn1_w_gate, ffn1_w_up, ffn1_w_down, mix_norm, ffn2_norm, ffn2_w_gate,
           ffn2_w_up, ffn2_w_down, ev_w_in, ev_b_f, ev_conv_w, ev_conv_b, ev_conv_norm, ev_q_norm,
           ev_k_norm, ev_w_out, od_w_in, od_conv_w, od_w_out):
    bsz, seq, d = x.shape
    assert d == D_MODEL and x.dtype == F32
    n_tok = bsz * seq
    depth = ffn1_norm.shape[0]
    for layer in range(depth):
        x = _ffn(x.reshape(n_tok, d), ffn1_norm[layer], ffn1_w_gate, ffn1_w_up, ffn1_w_down,
                 layer).reshape(bsz, seq, d)
        i = layer // 2
        mixer_out = None
        if layer % 2 == 0:
            a, qt, kx, vx = _even_in(x, mix_norm[layer], ev_w_in, i, ev_b_f[i], ev_conv_w[i],
                                     ev_conv_b[i], ev_conv_norm[i], ev_q_norm[i], ev_k_norm[i])
            o = _attention(qt, kx, vx)
            mixer_out = (a.reshape(n_tok, D_CONV), o.reshape(n_tok, D_ATTN), ev_w_out, i)
        else:
            x = _odd_mixer(x, mix_norm[layer], od_w_in, od_conv_w[i], od_w_out, i)
        x = _ffn(x.reshape(n_tok, d), ffn2_norm[layer], ffn2_w_gate, ffn2_w_up, ffn2_w_down,
                 layer, mixer_out).reshape(bsz, seq, d)
    return x
```

```python
import functools
import math

import jax
import jax.numpy as jnp
import numpy as np
from jax import lax
from jax.experimental import pallas as pl
from jax.experimental.pallas import tpu as pltpu

F32 = jnp.float32
BF16 = jnp.bfloat16

D_MODEL = 1024
D_FF = 2816
FFN_RES = 0.5
D_CONV = 512
CONV_A_WIDTH = 31
N_HEADS = 8
HEAD_DIM = 64
D_ATTN = N_HEADS * HEAD_DIM
CONV_C_WIDTH = 3
D_SHORT = 1024
EPS = 1e-6

LANES = 128
SUBLANES = 8
F_PAD = LANES
D_IN_EVEN = 2 * D_CONV + 3 * D_ATTN + N_HEADS

FFN_ROWS = 512
FF_CHUNK = 256
TM_MIX = 512
HALO_A = 32
CONV_ROWS = 64
CONV_CH = 256
HALO_C = 8
BQ = 512
KH = 256
PIPE_AHEAD = 2
LOG2E = 1.4426950408889634
BK = 512
KX_W = 2 * LANES
V_ROWS = 96
V7X_VMEM_BYTES = 64 * 1024 * 1024
VMEM_LIMIT = V7X_VMEM_BYTES * 7 // 8


def _const_spec(shape):
    nd = len(shape)
    return pl.BlockSpec(shape, lambda *_: (0,) * nd, pipeline_mode=pl.Buffered(1))


def _layer_spec(stacked, layer):
    _, rows, cols = stacked.shape
    return pl.BlockSpec((None, rows, cols), lambda *_: (layer, 0, 0), pipeline_mode=pl.Buffered(1))


def _rms_scale(x):
    return lax.rsqrt(jnp.mean(x * x, axis=-1, keepdims=True) + EPS)


def _ffn_kernel(*refs, mixer_out):
    if mixer_out:
        x_ref, a_ref, att_ref, wo_ref, g_ref, wg_ref, wu_ref, wd_ref, o_ref, h_ref = refs
    else:
        x_ref, g_ref, wg_ref, wu_ref, wd_ref, o_ref, h_ref = refs
    for r0 in range(0, x_ref.shape[0], FFN_ROWS):
        rows = slice(r0, r0 + FFN_ROWS)
        x = x_ref[rows, :]
        if mixer_out:
            x = (x + jnp.dot(a_ref[rows, :], wo_ref[0:D_CONV, :].astype(BF16),
                             preferred_element_type=F32)
                 + jnp.dot(att_ref[rows, :], wo_ref[D_CONV:, :].astype(BF16),
                           preferred_element_type=F32))
        xg = (x * g_ref[...]).astype(BF16)
        scale = _rms_scale(x)
        for c in range(D_FF // FF_CHUNK):
            sl = slice(c * FF_CHUNK, (c + 1) * FF_CHUNK)
            g = scale * jnp.dot(xg, wg_ref[:, sl].astype(BF16), preferred_element_type=F32)
            u = scale * jnp.dot(xg, wu_ref[:, sl].astype(BF16), preferred_element_type=F32)
            h_ref[rows, sl] = (g * jax.nn.sigmoid(g) * u).astype(BF16)
        y = jnp.dot(h_ref[rows, :], wd_ref[...].astype(BF16), preferred_element_type=F32)
        o_ref[rows, :] = x + FFN_RES * y


def _ffn(x2d, gain, w_gate, w_up, w_down, layer, mixer_out=None):
    n_tok = x2d.shape[0]
    tm = FFN_ROWS if mixer_out is not None else 2 * FFN_ROWS
    assert n_tok % tm == 0
    assert w_gate.shape[1:] == w_up.shape[1:] == (D_MODEL, D_FF) and w_down.shape[1:] == (D_FF, D_MODEL)
    row = lambda i: (i, 0)
    args = [x2d]
    in_specs = [pl.BlockSpec((tm, D_MODEL), row)]
    if mixer_out is not None:
        a2d, o2d, w_out, idx = mixer_out
        args += [a2d, o2d, w_out]
        in_specs += [
            pl.BlockSpec((tm, D_CONV), row),
            pl.BlockSpec((tm, D_ATTN), row),
            _layer_spec(w_out, idx),
        ]
    args += [gain.reshape(1, D_MODEL), w_gate, w_up, w_down]
    in_specs += [
        _const_spec((1, D_MODEL)),
        _layer_spec(w_gate, layer),
        _layer_spec(w_up, layer),
        _layer_spec(w_down, layer),
    ]
    return pl.pallas_call(
        functools.partial(_ffn_kernel, mixer_out=mixer_out is not None),
        grid=(n_tok // tm,),
        in_specs=in_specs,
        out_specs=pl.BlockSpec((tm, D_MODEL), row),
        out_shape=jax.ShapeDtypeStruct((n_tok, D_MODEL), F32),
        scratch_shapes=[pltpu.VMEM((tm, D_FF), BF16)],
        compiler_params=pltpu.CompilerParams(
            dimension_semantics=("arbitrary",), vmem_limit_bytes=VMEM_LIMIT),
        name="ffn_mix" if mixer_out is not None else "ffn",
    )(*args)


def _split3_bf16(x):
    hi = x.astype(BF16)
    r1 = x - hi.astype(F32)
    mid = r1.astype(BF16)
    lo = (r1 - mid.astype(F32)).astype(BF16)
    return hi, mid, lo


def _even_in_kernel(x_ref, g_ref, w_ref, wf_ref, bf_ref, cw_ref, cb_ref, cn_ref, qg_ref, kg_ref,
                    grp_ref, tri_ref, sel_ref, a_ref, qt_ref, kx_ref, vx_ref, abuf, fcarry, sbuf,
                    cbuf):
    i = pl.program_id(1)
    tm = x_ref.shape[1]
    x = x_ref[0]
    h = (x * _rms_scale(x) * g_ref[...]).astype(BF16)

    def proj(lo, width):
        return jnp.dot(h, w_ref[:, lo:lo + width].astype(BF16), preferred_element_type=F32)

    @pl.when(i == 0)
    def _():
        abuf[0:HALO_A, :] = jnp.zeros((HALO_A, D_CONV), F32)
        fcarry[...] = jnp.zeros_like(fcarry)

    @pl.when(i > 0)
    def _():
        abuf[0:HALO_A, :] = abuf[tm:tm + HALO_A, :]

    base = HALO_A - (CONV_A_WIDTH - 1)

    def glu(cb):
        cs = slice(cb * CONV_CH, (cb + 1) * CONV_CH)
        u = proj(cb * CONV_CH, CONV_CH)
        gate = proj(D_CONV + cb * CONV_CH, CONV_CH)
        abuf[HALO_A:HALO_A + tm, cs] = u * jax.nn.sigmoid(gate)

    def conv_block(cb):
        cs = slice(cb * CONV_CH, (cb + 1) * CONV_CH)
        for shift in range(1, SUBLANES):
            sbuf[shift - 1, :, cs] = abuf[shift:shift + tm + HALO_A - SUBLANES, cs]
        for r0 in range(0, tm, CONV_ROWS):
            conv = jnp.zeros((CONV_ROWS, CONV_CH), F32) + cb_ref[:, cs]
            for t in range(CONV_A_WIDTH):
                shift = (base + t) % SUBLANES
                lo = base + t - shift + r0
                rows = (sbuf[shift - 1, lo:lo + CONV_ROWS, cs] if shift
                        else abuf[lo:lo + CONV_ROWS, cs])
                conv = conv + cw_ref[t:t + 1, cs] * rows
            cbuf[r0:r0 + CONV_ROWS, cs] = conv

    def head_norm(z, gain_ref):
        ss = jnp.dot((z * z).astype(BF16), grp_ref[...], preferred_element_type=F32)
        return z * lax.rsqrt(ss * (1.0 / HEAD_DIM) + EPS) * gain_ref[...]

    def q_part():
        qt_ref[0] = head_norm(proj(2 * D_CONV, D_ATTN), qg_ref).T.astype(BF16)

    def k_part():
        kn = head_norm(proj(2 * D_CONV + D_ATTN, D_ATTN), kg_ref).astype(BF16)
        for p in range(N_HEADS // 2):
            kx_ref[0, :, p * KX_W:p * KX_W + LANES] = kn[:, p * LANES:(p + 1) * LANES]

    def v_part():
        vt = proj(2 * D_CONV + 2 * D_ATTN, D_ATTN).T
        ones = jnp.ones((V_ROWS - HEAD_DIM, tm), BF16)
        for hd in range(N_HEADS):
            vx_ref[0, hd, 0, 0:HEAD_DIM, :] = vt[hd * HEAD_DIM:(hd + 1) * HEAD_DIM, :].astype(BF16)
            vx_ref[0, hd, 0, HEAD_DIM:, :] = ones

    def decay_part():
        fl = jnp.dot(h, wf_ref[...].astype(BF16), preferred_element_type=F32) + bf_ref[...]
        logf = jnp.minimum(fl, 0.0) - jnp.log1p(jnp.exp(-jnp.abs(fl)))
        hi, mid, lo = _split3_bf16(logf)
        tri = tri_ref[...]
        cum = (jnp.dot(tri, hi, preferred_element_type=F32)
               + jnp.dot(tri, mid, preferred_element_type=F32)
               + jnp.dot(tri, lo, preferred_element_type=F32)) + fcarry[...]
        fcarry[...] = cum[tm - 1:tm, :]
        pieces = jnp.concatenate(_split3_bf16(cum * LOG2E), axis=1)
        ext = jnp.dot(pieces, sel_ref[...], preferred_element_type=F32).astype(BF16)
        for p in range(N_HEADS // 2):
            kx_ref[0, :, p * KX_W + LANES:(p + 1) * KX_W] = ext[:, p * LANES:(p + 1) * LANES]

    glu(0)
    glu(1)
    conv_block(0)
    q_part()
    k_part()
    v_part()
    decay_part()
    conv_block(1)
    conv = cbuf[...]
    ss = jnp.dot((conv * conv).astype(BF16), jnp.ones((D_CONV, LANES), BF16),
                 preferred_element_type=F32)[:, 0:1]
    an = conv * lax.rsqrt(ss * (1.0 / D_CONV) + EPS) * cn_ref[...]
    a_ref[0] = (an * jax.nn.sigmoid(an)).astype(BF16)


def _decay_selector():
    sel = np.zeros((3 * F_PAD, (N_HEADS // 2) * LANES), np.float32)
    for piece in range(3):
        for hd in range(N_HEADS):
            sel[piece * F_PAD + hd, (hd // 2) * LANES + 3 * (hd % 2) + piece] = 1.0
    return jnp.asarray(sel, BF16)


def _even_in(x, gain, w_in, idx, b_f, conv_w, conv_b, conv_norm, q_norm, k_norm):
    bsz, seq, _ = x.shape
    tm = BK
    assert seq % tm == 0 and BK == BQ and w_in.shape[1:] == (D_MODEL, D_IN_EVEN)
    n_pairs = N_HEADS // 2
    scale = LOG2E / math.sqrt(HEAD_DIM)
    w_f = jnp.pad(w_in[idx, :, D_IN_EVEN - N_HEADS:], ((0, 0), (0, F_PAD - N_HEADS)))
    bf_pad = jnp.pad(b_f, (0, F_PAD - N_HEADS)).reshape(1, F_PAD)
    cw_pad = jnp.pad(conv_w, ((0, 1), (0, 0)))
    qg = (jnp.tile(q_norm, N_HEADS) * scale).reshape(1, D_ATTN)
    kg = jnp.tile(k_norm, N_HEADS).reshape(1, D_ATTN)
    head_of = jnp.arange(D_ATTN) // HEAD_DIM
    grp = (head_of[:, None] == head_of[None, :]).astype(BF16)
    tri = (jnp.arange(tm)[:, None] >= jnp.arange(tm)[None, :]).astype(BF16)
    tok = lambda b, i: (b, i, 0)
    return pl.pallas_call(
        _even_in_kernel,
        grid=(bsz, seq // tm),
        in_specs=[
            pl.BlockSpec((1, tm, D_MODEL), tok),
            _const_spec((1, D_MODEL)),
            _layer_spec(w_in, idx),
            _const_spec((D_MODEL, F_PAD)),
            _const_spec((1, F_PAD)),
            _const_spec((CONV_A_WIDTH + 1, D_CONV)),
            _const_spec((1, D_CONV)),
            _const_spec((1, D_CONV)),
            _const_spec((1, D_ATTN)),
            _const_spec((1, D_ATTN)),
            _const_spec((D_ATTN, D_ATTN)),
            _const_spec((tm, tm)),
            _const_spec((3 * F_PAD, n_pairs * LANES)),
        ],
        out_specs=[
            pl.BlockSpec((1, tm, D_CONV), tok),
            pl.BlockSpec((1, D_ATTN, tm), lambda b, i: (b, 0, i)),
            pl.BlockSpec((1, tm, n_pairs * KX_W), tok),
            pl.BlockSpec((1, N_HEADS, 1, V_ROWS, tm), lambda b, i: (b, 0, i, 0, 0)),
        ],
        out_shape=[
            jax.ShapeDtypeStruct((bsz, seq, D_CONV), BF16),
            jax.ShapeDtypeStruct((bsz, D_ATTN, seq), BF16),
            jax.ShapeDtypeStruct((bsz, seq, n_pairs * KX_W), BF16),
            jax.ShapeDtypeStruct((bsz, N_HEADS, seq // tm, V_ROWS, tm), BF16),
        ],
        scratch_shapes=[pltpu.VMEM((tm + HALO_A, D_CONV), F32), pltpu.VMEM((1, F_PAD), F32),
                        pltpu.VMEM((SUBLANES - 1, tm + HALO_A - SUBLANES, D_CONV), F32),
                        pltpu.VMEM((tm, D_CONV), F32)],
        compiler_params=pltpu.CompilerParams(
            dimension_semantics=("arbitrary", "arbitrary"), vmem_limit_bytes=VMEM_LIMIT),
        name="even_in",
    )(x, gain.reshape(1, D_MODEL), w_in, w_f, bf_pad, cw_pad, conv_b.reshape(1, D_CONV),
      conv_norm.reshape(1, D_CONV), qg, kg, grp, tri, _decay_selector())


def _attn_kernel(qt_ref, kx_ref, vx_ref, o_ref, acc_ref):
    seq = kx_ref.shape[1]
    nq = seq // BQ
    zeros = jnp.zeros((HEAD_DIM, BQ), BF16)
    r = lax.broadcasted_iota(jnp.int32, (LANES, BQ), 0)
    ext = [jnp.where((r >= 3 * hd) & (r < 3 * hd + 3), -1.0, 0.0).astype(BF16) for hd in range(2)]
    key = lax.broadcasted_iota(jnp.int32, (KH, KH), 0)
    qry = lax.broadcasted_iota(jnp.int32, (KH, KH), 1)
    causal = key <= qry
    rhs_cache = {}

    def rhs(i, hd):
        if (i, hd) not in rhs_cache:
            qt = qt_ref[0, :, i * BQ:(i + 1) * BQ]
            top = [qt[0:HEAD_DIM], zeros] if hd == 0 else [zeros, qt[HEAD_DIM:]]
            rhs_cache[(i, hd)] = jnp.concatenate(top + [ext[hd]], axis=0)
        return rhs_cache[(i, hd)]

    def logits(i, j, kh, hd):
        kx = kx_ref[0, j * BK + kh * KH:j * BK + (kh + 1) * KH, :]
        if j < i:
            return jnp.dot(kx, rhs(i, hd), preferred_element_type=F32)
        if kh == 0:
            st = jnp.dot(kx, rhs(i, hd), preferred_element_type=F32)
            return jnp.concatenate([jnp.where(causal, st[:, :KH], -jnp.inf), st[:, KH:]], axis=1)
        st = jnp.dot(kx, rhs(i, hd)[:, KH:], preferred_element_type=F32)
        return jnp.where(causal, st, -jnp.inf)

    units = [(i, j, kh, hd) for i in range(nq) for j in range(i + 1) for kh in range(2)
             for hd in range(2)]
    m = [None, None]
    sts = {k: logits(*units[k]) for k in range(min(PIPE_AHEAD, len(units)))}
    for k, (i, j, kh, hd) in enumerate(units):
        if k + PIPE_AHEAD < len(units):
            sts[k + PIPE_AHEAD] = logits(*units[k + PIPE_AHEAD])
        st = sts.pop(k)
        slot = i % 2
        first = j == 0 and kh == 0
        right_only = j == i and kh == 1
        cols = slice(KH, BQ) if right_only else slice(0, BQ)
        col_max = jnp.max(st, axis=0, keepdims=True)
        if first:
            m_new = col_max
        else:
            m_prev = m[hd][:, cols]
            m_new = jnp.maximum(m_prev, col_max)
            alpha = jnp.exp2(m_prev - m_new)
        pt = jnp.exp2(st - m_new).astype(BF16)
        vx = vx_ref[0, hd, j, :, kh * KH:(kh + 1) * KH]
        pv = jnp.dot(vx, pt, preferred_element_type=F32)
        if first:
            acc_ref[slot, hd] = pv
            m[hd] = m_new
        elif right_only:
            acc_ref[slot, hd, :, KH:] = acc_ref[slot, hd, :, KH:] * alpha + pv
            m[hd] = jnp.concatenate([m[hd][:, :KH], m_new], axis=1)
        else:
            acc_ref[slot, hd] = acc_ref[slot, hd] * alpha + pv
            m[hd] = m_new
        if right_only and hd == 1:
            parts = []
            for h2 in range(2):
                acc = acc_ref[slot, h2]
                parts.append(acc[0:HEAD_DIM] * (1.0 / acc[HEAD_DIM:HEAD_DIM + 1]))
            o_ref[0, i * BQ:(i + 1) * BQ, :] = jnp.concatenate(parts, axis=0).T.astype(BF16)


def _attention(qt, kx, vx):
    bsz, _, seq = qt.shape
    n_pairs = N_HEADS // 2
    return pl.pallas_call(
        _attn_kernel,
        grid=(bsz, n_pairs),
        in_specs=[
            pl.BlockSpec((1, 2 * HEAD_DIM, seq), lambda b, p: (b, p, 0)),
            pl.BlockSpec((1, seq, KX_W), lambda b, p: (b, 0, p)),
            pl.BlockSpec((1, 2, seq // BK, V_ROWS, BK), lambda b, p: (b, p, 0, 0, 0)),
        ],
        out_specs=pl.BlockSpec((1, seq, LANES), lambda b, p: (b, 0, p)),
        out_shape=jax.ShapeDtypeStruct((bsz, seq, D_ATTN), BF16),
        scratch_shapes=[pltpu.VMEM((2, 2, V_ROWS, BQ), F32)],
        compiler_params=pltpu.CompilerParams(
            dimension_semantics=("arbitrary", "arbitrary"), vmem_limit_bytes=VMEM_LIMIT),
        name="fox_attention",
    )(qt, kx, vx)


def _odd_kernel(x_ref, g_ref, wi_ref, cw_ref, wo_ref, y_ref, mbuf):
    i = pl.program_id(1)
    tm = TM_MIX
    base = HALO_C - (CONV_C_WIDTH - 1)
    for sub in range(x_ref.shape[1] // tm):
        rows = slice(sub * tm, (sub + 1) * tm)
        x = x_ref[0, rows, :]
        h = (x * _rms_scale(x) * g_ref[...]).astype(BF16)

        def proj(k):
            return jnp.dot(h, wi_ref[:, k * D_SHORT:(k + 1) * D_SHORT].astype(BF16),
                           preferred_element_type=F32)

        if sub == 0:
            @pl.when(i == 0)
            def _():
                mbuf[0:HALO_C, :] = jnp.zeros((HALO_C, D_SHORT), F32)

            @pl.when(i > 0)
            def _():
                mbuf[0:HALO_C, :] = mbuf[tm:tm + HALO_C, :]
        else:
            mbuf[0:HALO_C, :] = mbuf[tm:tm + HALO_C, :]

        mbuf[HALO_C:HALO_C + tm, :] = proj(1) * proj(2)
        conv = cw_ref[0:1, :] * mbuf[base:base + tm, :]
        for t in range(1, CONV_C_WIDTH):
            conv = conv + cw_ref[t:t + 1, :] * mbuf[base + t:base + t + tm, :]
        y = (proj(0) * conv).astype(BF16)
        y_ref[0, rows, :] = x + jnp.dot(y, wo_ref[...].astype(BF16), preferred_element_type=F32)


def _odd_mixer(x, gain, w_in, conv_w, w_out, idx):
    bsz, seq, _ = x.shape
    tm = 2 * TM_MIX
    assert seq % tm == 0 and w_in.shape[1:] == (D_MODEL, 3 * D_SHORT)
    tok = lambda b, i: (b, i, 0)
    return pl.pallas_call(
        _odd_kernel,
        grid=(bsz, seq // tm),
        in_specs=[
            pl.BlockSpec((1, tm, D_MODEL), tok),
            _const_spec((1, D_MODEL)),
            _layer_spec(w_in, idx),
            _const_spec((CONV_C_WIDTH + 1, D_SHORT)),
            _layer_spec(w_out, idx),
        ],
        out_specs=pl.BlockSpec((1, tm, D_MODEL), tok),
        out_shape=jax.ShapeDtypeStruct((bsz, seq, D_MODEL), F32),
        scratch_shapes=[pltpu.VMEM((TM_MIX + HALO_C, D_SHORT), F32)],
        compiler_params=pltpu.CompilerParams(
            dimension_semantics=("arbitrary", "arbitrary"), vmem_limit_bytes=VMEM_LIMIT),
        name="odd_mixer",
    )(x, gain.reshape(1, D_MODEL), w_in, jnp.pad(conv_w, ((0, 1), (0, 0))), w_out)


def kernel(x, ffn1_norm, ffn1_w_gate, ffn1_w_up, ffn1_w_down, mix_norm, ffn2_norm, ffn2_w_gate,
           ffn2_w_up, ffn2_w_down, ev_w_in, ev_b_f, ev_conv_w, ev_conv_b, ev_conv_norm, ev_q_norm,
           ev_k_norm, ev_w_out, od_w_in, od_conv_w, od_w_out):
    bsz, seq, d = x.shape
    assert d == D_MODEL and x.dtype == F32
    n_tok = bsz * seq
    depth = ffn1_norm.shape[0]
    for layer in range(depth):
        x = _ffn(x.reshape(n_tok, d), ffn1_norm[layer], ffn1_w_gate, ffn1_w_up, ffn1_w_down,
                 layer).reshape(bsz, seq, d)
        i = layer // 2
        mixer_out = None
        if layer % 2 == 0:
            a, qt, kx, vx = _even_in(x, mix_norm[layer], ev_w_in, i, ev_b_f[i], ev_conv_w[i],
                                     ev_conv_b[i], ev_conv_norm[i], ev_q_norm[i], ev_k_norm[i])
            o = _attention(qt, kx, vx)
            mixer_out = (a.reshape(n_tok, D_CONV), o.reshape(n_tok, D_ATTN), ev_w_out, i)
        else:
            x = _odd_mixer(x, mix_norm[layer], od_w_in, od_conv_w[i], od_w_out, i)
        x = _ffn(x.reshape(n_tok, d), ffn2_norm[layer], ffn2_w_gate, ffn2_w_up, ffn2_w_down,
                 layer, mixer_out).reshape(bsz, seq, d)
    return x
```

```python
import functools
import math

import jax
import jax.numpy as jnp
import numpy as np
from jax import lax
from jax.experimental import pallas as pl
from jax.experimental.pallas import tpu as pltpu

F32 = jnp.float32
BF16 = jnp.bfloat16

D_MODEL = 1024
D_FF = 2816
FFN_RES = 0.5
D_CONV = 512
CONV_A_WIDTH = 31
N_HEADS = 8
HEAD_DIM = 64
D_ATTN = N_HEADS * HEAD_DIM
CONV_C_WIDTH = 3
D_SHORT = 1024
EPS = 1e-6

LANES = 128
SUBLANES = 8
F_PAD = LANES
D_IN_EVEN = 2 * D_CONV + 3 * D_ATTN + N_HEADS

FFN_ROWS = 512
FF_CHUNK = 256
TM_MIX = 512
HALO_A = 32
CONV_ROWS = 64
CONV_CH = 256
HALO_C = 8
BQ = 512
KH = 256
PIPE_AHEAD = 2
LOG2E = 1.4426950408889634
BK = 512
KX_W = 2 * LANES
V_ROWS = 96
V7X_VMEM_BYTES = 64 * 1024 * 1024
VMEM_LIMIT = V7X_VMEM_BYTES * 7 // 8


def _const_spec(shape):
    nd = len(shape)
    return pl.BlockSpec(shape, lambda *_: (0,) * nd, pipeline_mode=pl.Buffered(1))


def _layer_spec(stacked, layer):
    _, rows, cols = stacked.shape
    return pl.BlockSpec((None, rows, cols), lambda *_: (layer, 0, 0), pipeline_mode=pl.Buffered(1))


def _rms_scale(x):
    return lax.rsqrt(jnp.mean(x * x, axis=-1, keepdims=True) + EPS)


def _ffn_kernel(*refs, mixer_out):
    if mixer_out:
        x_ref, a_ref, att_ref, wo_ref, g_ref, wg_ref, wu_ref, wd_ref, o_ref, h_ref = refs
    else:
        x_ref, g_ref, wg_ref, wu_ref, wd_ref, o_ref, h_ref = refs
    for r0 in range(0, x_ref.shape[0], FFN_ROWS):
        rows = slice(r0, r0 + FFN_ROWS)
        x = x_ref[rows, :]
        if mixer_out:
            x = x + jnp.dot(att_ref[rows, :], wo_ref[D_CONV:, :].astype(BF16),
                            preferred_element_type=F32)
            an = a_ref[rows, :]
            x = x + jnp.dot((an * jax.nn.sigmoid(an)).astype(BF16),
                            wo_ref[0:D_CONV, :].astype(BF16), preferred_element_type=F32)
        xg = (x * g_ref[...]).astype(BF16)
        scale = _rms_scale(x)
        for c in range(D_FF // FF_CHUNK):
            sl = slice(c * FF_CHUNK, (c + 1) * FF_CHUNK)
            g = scale * jnp.dot(xg, wg_ref[:, sl].astype(BF16), preferred_element_type=F32)
            u = scale * jnp.dot(xg, wu_ref[:, sl].astype(BF16), preferred_element_type=F32)
            h_ref[rows, sl] = (g * jax.nn.sigmoid(g) * u).astype(BF16)
        y = jnp.dot(h_ref[rows, :], wd_ref[...].astype(BF16), preferred_element_type=F32)
        o_ref[rows, :] = x + FFN_RES * y


def _ffn(x2d, gain, w_gate, w_up, w_down, layer, mixer_out=None):
    n_tok = x2d.shape[0]
    tm = FFN_ROWS if mixer_out is not None else 2 * FFN_ROWS
    assert n_tok % tm == 0
    assert w_gate.shape[1:] == w_up.shape[1:] == (D_MODEL, D_FF) and w_down.shape[1:] == (D_FF, D_MODEL)
    row = lambda i: (i, 0)
    args = [x2d]
    in_specs = [pl.BlockSpec((tm, D_MODEL), row)]
    if mixer_out is not None:
        a2d, o2d, w_out, idx = mixer_out
        args += [a2d, o2d, w_out]
        in_specs += [
            pl.BlockSpec((tm, D_CONV), row),
            pl.BlockSpec((tm, D_ATTN), row),
            _layer_spec(w_out, idx),
        ]
    args += [gain.reshape(1, D_MODEL), w_gate, w_up, w_down]
    in_specs += [
        _const_spec((1, D_MODEL)),
        _layer_spec(w_gate, layer),
        _layer_spec(w_up, layer),
        _layer_spec(w_down, layer),
    ]
    return pl.pallas_call(
        functools.partial(_ffn_kernel, mixer_out=mixer_out is not None),
        grid=(n_tok // tm,),
        in_specs=in_specs,
        out_specs=pl.BlockSpec((tm, D_MODEL), row),
        out_shape=jax.ShapeDtypeStruct((n_tok, D_MODEL), F32),
        scratch_shapes=[pltpu.VMEM((tm, D_FF), BF16)],
        compiler_params=pltpu.CompilerParams(
            dimension_semantics=("arbitrary",), vmem_limit_bytes=VMEM_LIMIT),
        name="ffn_mix" if mixer_out is not None else "ffn",
    )(*args)


def _split3_bf16(x):
    hi = x.astype(BF16)
    r1 = x - hi.astype(F32)
    mid = r1.astype(BF16)
    lo = (r1 - mid.astype(F32)).astype(BF16)
    return hi, mid, lo


def _even_in_kernel(x_ref, g_ref, w_ref, wf_ref, bf_ref, cw_ref, cb_ref, cn_ref, qg_ref, kg_ref,
                    grp_ref, tri_ref, sel_ref, a_ref, qt_ref, kx_ref, vx_ref, abuf, fcarry, sbuf,
                    cbuf):
    i = pl.program_id(1)
    tm = x_ref.shape[1]
    x = x_ref[0]
    h = (x * _rms_scale(x) * g_ref[...]).astype(BF16)

    def proj(lo, width):
        return jnp.dot(h, w_ref[:, lo:lo + width].astype(BF16), preferred_element_type=F32)

    @pl.when(i == 0)
    def _():
        abuf[0:HALO_A, :] = jnp.zeros((HALO_A, D_CONV), F32)
        fcarry[...] = jnp.zeros_like(fcarry)

    @pl.when(i > 0)
    def _():
        abuf[0:HALO_A, :] = abuf[tm:tm + HALO_A, :]

    base = HALO_A - (CONV_A_WIDTH - 1)

    def glu(cb):
        cs = slice(cb * CONV_CH, (cb + 1) * CONV_CH)
        u = proj(cb * CONV_CH, CONV_CH)
        gate = proj(D_CONV + cb * CONV_CH, CONV_CH)
        abuf[HALO_A:HALO_A + tm, cs] = u * jax.nn.sigmoid(gate)

    def conv_block(cb):
        cs = slice(cb * CONV_CH, (cb + 1) * CONV_CH)
        for shift in range(1, SUBLANES):
            sbuf[shift - 1, :, cs] = abuf[shift:shift + tm + HALO_A - SUBLANES, cs]
        for r0 in range(0, tm, CONV_ROWS):
            conv = jnp.zeros((CONV_ROWS, CONV_CH), F32) + cb_ref[:, cs]
            for t in range(CONV_A_WIDTH):
                shift = (base + t) % SUBLANES
                lo = base + t - shift + r0
                rows = (sbuf[shift - 1, lo:lo + CONV_ROWS, cs] if shift
                        else abuf[lo:lo + CONV_ROWS, cs])
                conv = conv + cw_ref[t:t + 1, cs] * rows
            cbuf[r0:r0 + CONV_ROWS, cs] = conv

    def head_norm(z, gain_ref):
        ss = jnp.dot((z * z).astype(BF16), grp_ref[...], preferred_element_type=F32)
        return z * lax.rsqrt(ss * (1.0 / HEAD_DIM) + EPS) * gain_ref[...]

    def q_part():
        qt_ref[0] = head_norm(proj(2 * D_CONV, D_ATTN), qg_ref).T.astype(BF16)

    def k_part():
        kn = head_norm(proj(2 * D_CONV + D_ATTN, D_ATTN), kg_ref).astype(BF16)
        for p in range(N_HEADS // 2):
            kx_ref[0, :, p * KX_W:p * KX_W + LANES] = kn[:, p * LANES:(p + 1) * LANES]

    def v_part():
        vt = proj(2 * D_CONV + 2 * D_ATTN, D_ATTN).T
        ones = jnp.ones((V_ROWS - HEAD_DIM, tm), BF16)
        for hd in range(N_HEADS):
            vx_ref[0, hd, 0, 0:HEAD_DIM, :] = vt[hd * HEAD_DIM:(hd + 1) * HEAD_DIM, :].astype(BF16)
            vx_ref[0, hd, 0, HEAD_DIM:, :] = ones

    def decay_part():
        fl = jnp.dot(h, wf_ref[...].astype(BF16), preferred_element_type=F32) + bf_ref[...]
        logf = jnp.minimum(fl, 0.0) - jnp.log1p(jnp.exp(-jnp.abs(fl)))
        hi, mid, lo = _split3_bf16(logf)
        tri = tri_ref[...]
        cum = (jnp.dot(tri, hi, preferred_element_type=F32)
               + jnp.dot(tri, mid, preferred_element_type=F32)
               + jnp.dot(tri, lo, preferred_element_type=F32)) + fcarry[...]
        fcarry[...] = cum[tm - 1:tm, :]
        pieces = jnp.concatenate(_split3_bf16(cum * LOG2E), axis=1)
        ext = jnp.dot(pieces, sel_ref[...], preferred_element_type=F32).astype(BF16)
        for p in range(N_HEADS // 2):
            kx_ref[0, :, p * KX_W + LANES:(p + 1) * KX_W] = ext[:, p * LANES:(p + 1) * LANES]

    glu(0)
    glu(1)
    conv_block(0)
    q_part()
    k_part()
    v_part()
    decay_part()
    conv_block(1)
    conv = cbuf[...]
    ss = jnp.dot((conv * conv).astype(BF16), jnp.ones((D_CONV, LANES), BF16),
                 preferred_element_type=F32)[:, 0:1]
    an = conv * lax.rsqrt(ss * (1.0 / D_CONV) + EPS) * cn_ref[...]
    a_ref[0] = an


def _decay_selector():
    sel = np.zeros((3 * F_PAD, (N_HEADS // 2) * LANES), np.float32)
    for piece in range(3):
        for hd in range(N_HEADS):
            sel[piece * F_PAD + hd, (hd // 2) * LANES + 3 * (hd % 2) + piece] = 1.0
    return jnp.asarray(sel, BF16)


def _even_in(x, gain, w_in, idx, b_f, conv_w, conv_b, conv_norm, q_norm, k_norm):
    bsz, seq, _ = x.shape
    tm = BK
    assert seq % tm == 0 and BK == BQ and w_in.shape[1:] == (D_MODEL, D_IN_EVEN)
    n_pairs = N_HEADS // 2
    scale = LOG2E / math.sqrt(HEAD_DIM)
    w_f = jnp.pad(w_in[idx, :, D_IN_EVEN - N_HEADS:], ((0, 0), (0, F_PAD - N_HEADS)))
    bf_pad = jnp.pad(b_f, (0, F_PAD - N_HEADS)).reshape(1, F_PAD)
    cw_pad = jnp.pad(conv_w, ((0, 1), (0, 0)))
    qg = (jnp.tile(q_norm, N_HEADS) * scale).reshape(1, D_ATTN)
    kg = jnp.tile(k_norm, N_HEADS).reshape(1, D_ATTN)
    head_of = jnp.arange(D_ATTN) // HEAD_DIM
    grp = (head_of[:, None] == head_of[None, :]).astype(BF16)
    tri = (jnp.arange(tm)[:, None] >= jnp.arange(tm)[None, :]).astype(BF16)
    tok = lambda b, i: (b, i, 0)
    return pl.pallas_call(
        _even_in_kernel,
        grid=(bsz, seq // tm),
        in_specs=[
            pl.BlockSpec((1, tm, D_MODEL), tok),
            _const_spec((1, D_MODEL)),
            _layer_spec(w_in, idx),
            _const_spec((D_MODEL, F_PAD)),
            _const_spec((1, F_PAD)),
            _const_spec((CONV_A_WIDTH + 1, D_CONV)),
            _const_spec((1, D_CONV)),
            _const_spec((1, D_CONV)),
            _const_spec((1, D_ATTN)),
            _const_spec((1, D_ATTN)),
            _const_spec((D_ATTN, D_ATTN)),
            _const_spec((tm, tm)),
            _const_spec((3 * F_PAD, n_pairs * LANES)),
        ],
        out_specs=[
            pl.BlockSpec((1, tm, D_CONV), tok),
            pl.BlockSpec((1, D_ATTN, tm), lambda b, i: (b, 0, i)),
            pl.BlockSpec((1, tm, n_pairs * KX_W), tok),
            pl.BlockSpec((1, N_HEADS, 1, V_ROWS, tm), lambda b, i: (b, 0, i, 0, 0)),
        ],
        out_shape=[
            jax.ShapeDtypeStruct((bsz, seq, D_CONV), F32),
            jax.ShapeDtypeStruct((bsz, D_ATTN, seq), BF16),
            jax.ShapeDtypeStruct((bsz, seq, n_pairs * KX_W), BF16),
            jax.ShapeDtypeStruct((bsz, N_HEADS, seq // tm, V_ROWS, tm), BF16),
        ],
        scratch_shapes=[pltpu.VMEM((tm + HALO_A, D_CONV), F32), pltpu.VMEM((1, F_PAD), F32),
                        pltpu.VMEM((SUBLANES - 1, tm + HALO_A - SUBLANES, D_CONV), F32),
                        pltpu.VMEM((tm, D_CONV), F32)],
        compiler_params=pltpu.CompilerParams(
            dimension_semantics=("arbitrary", "arbitrary"), vmem_limit_bytes=VMEM_LIMIT),
        name="even_in",
    )(x, gain.reshape(1, D_MODEL), w_in, w_f, bf_pad, cw_pad, conv_b.reshape(1, D_CONV),
      conv_norm.reshape(1, D_CONV), qg, kg, grp, tri, _decay_selector())


def _attn_kernel(qt_ref, kx_ref, vx_ref, o_ref, acc_ref):
    seq = kx_ref.shape[1]
    nq = seq // BQ
    zeros = jnp.zeros((HEAD_DIM, BQ), BF16)
    r = lax.broadcasted_iota(jnp.int32, (LANES, BQ), 0)
    ext = [jnp.where((r >= 3 * hd) & (r < 3 * hd + 3), -1.0, 0.0).astype(BF16) for hd in range(2)]
    key = lax.broadcasted_iota(jnp.int32, (KH, KH), 0)
    qry = lax.broadcasted_iota(jnp.int32, (KH, KH), 1)
    causal = key <= qry
    rhs_cache = {}

    def rhs(i, hd):
        if (i, hd) not in rhs_cache:
            qt = qt_ref[0, :, i * BQ:(i + 1) * BQ]
            top = [qt[0:HEAD_DIM], zeros] if hd == 0 else [zeros, qt[HEAD_DIM:]]
            rhs_cache[(i, hd)] = jnp.concatenate(top + [ext[hd]], axis=0)
        return rhs_cache[(i, hd)]

    def logits(i, j, kh, hd):
        kx = kx_ref[0, j * BK + kh * KH:j * BK + (kh + 1) * KH, :]
        if j < i:
            return jnp.dot(kx, rhs(i, hd), preferred_element_type=F32)
        if kh == 0:
            st = jnp.dot(kx, rhs(i, hd), preferred_element_type=F32)
            return jnp.concatenate([jnp.where(causal, st[:, :KH], -jnp.inf), st[:, KH:]], axis=1)
        st = jnp.dot(kx, rhs(i, hd)[:, KH:], preferred_element_type=F32)
        return jnp.where(causal, st, -jnp.inf)

    units = [(i, j, kh, hd) for i in range(nq) for j in range(i + 1) for kh in range(2)
             for hd in range(2)]
    m = [None, None]
    sts = {k: logits(*units[k]) for k in range(min(PIPE_AHEAD, len(units)))}
    for k, (i, j, kh, hd) in enumerate(units):
        if k + PIPE_AHEAD < len(units):
            sts[k + PIPE_AHEAD] = logits(*units[k + PIPE_AHEAD])
        st = sts.pop(k)
        slot = i % 2
        first = j == 0 and kh == 0
        right_only = j == i and kh == 1
        cols = slice(KH, BQ) if right_only else slice(0, BQ)
        col_max = jnp.max(st, axis=0, keepdims=True)
        if first:
            m_new = col_max
        else:
            m_prev = m[hd][:, cols]
            m_new = jnp.maximum(m_prev, col_max)
            alpha = jnp.exp2(m_prev - m_new)
        pt = jnp.exp2(st - m_new).astype(BF16)
        vx = vx_ref[0, hd, j, :, kh * KH:(kh + 1) * KH]
        pv = jnp.dot(vx, pt, preferred_element_type=F32)
        if first:
            acc_ref[slot, hd] = pv
            m[hd] = m_new
        elif right_only:
            acc_ref[slot, hd, :, KH:] = acc_ref[slot, hd, :, KH:] * alpha + pv
            m[hd] = jnp.concatenate([m[hd][:, :KH], m_new], axis=1)
        else:
            acc_ref[slot, hd] = acc_ref[slot, hd] * alpha + pv
            m[hd] = m_new
        if right_only and hd == 1:
            parts = []
            for h2 in range(2):
                acc = acc_ref[slot, h2]
                parts.append(acc[0:HEAD_DIM] * (1.0 / acc[HEAD_DIM:HEAD_DIM + 1]))
            o_ref[0, i * BQ:(i + 1) * BQ, :] = jnp.concatenate(parts, axis=0).T.astype(BF16)


def _attention(qt, kx, vx):
    bsz, _, seq = qt.shape
    n_pairs = N_HEADS // 2
    return pl.pallas_call(
        _attn_kernel,
        grid=(bsz, n_pairs),
        in_specs=[
            pl.BlockSpec((1, 2 * HEAD_DIM, seq), lambda b, p: (b, p, 0)),
            pl.BlockSpec((1, seq, KX_W), lambda b, p: (b, 0, p)),
            pl.BlockSpec((1, 2, seq // BK, V_ROWS, BK), lambda b, p: (b, p, 0, 0, 0)),
        ],
        out_specs=pl.BlockSpec((1, seq, LANES), lambda b, p: (b, 0, p)),
        out_shape=jax.ShapeDtypeStruct((bsz, seq, D_ATTN), BF16),
        scratch_shapes=[pltpu.VMEM((2, 2, V_ROWS, BQ), F32)],
        compiler_params=pltpu.CompilerParams(
            dimension_semantics=("arbitrary", "arbitrary"), vmem_limit_bytes=VMEM_LIMIT),
        name="fox_attention",
    )(qt, kx, vx)


def _odd_kernel(x_ref, g_ref, wi_ref, cw_ref, wo_ref, y_ref, mbuf):
    i = pl.program_id(1)
    tm = TM_MIX
    base = HALO_C - (CONV_C_WIDTH - 1)
    for sub in range(x_ref.shape[1] // tm):
        rows = slice(sub * tm, (sub + 1) * tm)
        x = x_ref[0, rows, :]
        h = (x * _rms_scale(x) * g_ref[...]).astype(BF16)

        def proj(k):
            return jnp.dot(h, wi_ref[:, k * D_SHORT:(k + 1) * D_SHORT].astype(BF16),
                           preferred_element_type=F32)

        if sub == 0:
            @pl.when(i == 0)
            def _():
                mbuf[0:HALO_C, :] = jnp.zeros((HALO_C, D_SHORT), F32)

            @pl.when(i > 0)
            def _():
                mbuf[0:HALO_C, :] = mbuf[tm:tm + HALO_C, :]
        else:
            mbuf[0:HALO_C, :] = mbuf[tm:tm + HALO_C, :]

        mbuf[HALO_C:HALO_C + tm, :] = proj(1) * proj(2)
        conv = cw_ref[0:1, :] * mbuf[base:base + tm, :]
        for t in range(1, CONV_C_WIDTH):
            conv = conv + cw_ref[t:t + 1, :] * mbuf[base + t:base + t + tm, :]
        y = (proj(0) * conv).astype(BF16)
        y_ref[0, rows, :] = x + jnp.dot(y, wo_ref[...].astype(BF16), preferred_element_type=F32)


def _odd_mixer(x, gain, w_in, conv_w, w_out, idx):
    bsz, seq, _ = x.shape
    tm = 2 * TM_MIX
    assert seq % tm == 0 and w_in.shape[1:] == (D_MODEL, 3 * D_SHORT)
    tok = lambda b, i: (b, i, 0)
    return pl.pallas_call(
        _odd_kernel,
        grid=(bsz, seq // tm),
        in_specs=[
            pl.BlockSpec((1, tm, D_MODEL), tok),
            _const_spec((1, D_MODEL)),
            _layer_spec(w_in, idx),
            _const_spec((CONV_C_WIDTH + 1, D_SHORT)),
            _layer_spec(w_out, idx),
        ],
        out_specs=pl.BlockSpec((1, tm, D_MODEL), tok),
        out_shape=jax.ShapeDtypeStruct((bsz, seq, D_MODEL), F32),
        scratch_shapes=[pltpu.VMEM((TM_MIX + HALO_C, D_SHORT), F32)],
        compiler_params=pltpu.CompilerParams(
            dimension_semantics=("arbitrary", "arbitrary"), vmem_limit_bytes=VMEM_LIMIT),
        name="odd_mixer",
    )(x, gain.reshape(1, D_MODEL), w_in, jnp.pad(conv_w, ((0, 1), (0, 0))), w_out)


def kernel(x, ffn1_norm, ffn1_w_gate, ffn1_w_up, ffn1_w_down, mix_norm, ffn2_norm, ffn2_w_gate,
           ffn2_w_up, ffn2_w_down, ev_w_in, ev_b_f, ev_conv_w, ev_conv_b, ev_conv_norm, ev_q_norm,
           ev_k_norm, ev_w_out, od_w_in, od_conv_w, od_w_out):
    bsz, seq, d = x.shape
    assert d == D_MODEL and x.dtype == F32
    n_tok = bsz * seq
    depth = ffn1_norm.shape[0]
    for layer in range(depth):
        x = _ffn(x.reshape(n_tok, d), ffn1_norm[layer], ffn1_w_gate, ffn1_w_up, ffn1_w_down,
                 layer).reshape(bsz, seq, d)
        i = layer // 2
        mixer_out = None
        if layer % 2 == 0:
            a, qt, kx, vx = _even_in(x, mix_norm[layer], ev_w_in, i, ev_b_f[i], ev_conv_w[i],
                                     ev_conv_b[i], ev_conv_norm[i], ev_q_norm[i], ev_k_norm[i])
            o = _attention(qt, kx, vx)
            mixer_out = (a.reshape(n_tok, D_CONV), o.reshape(n_tok, D_ATTN), ev_w_out, i)
        else:
            x = _odd_mixer(x, mix_norm[layer], od_w_in, od_conv_w[i], od_w_out, i)
        x = _ffn(x.reshape(n_tok, d), ffn2_norm[layer], ffn2_w_gate, ffn2_w_up, ffn2_w_down,
                 layer, mixer_out).reshape(bsz, seq, d)
    return x
```
